```python
import jax, jax.numpy as jnp
from jax import lax
import numpy as np

D_MODEL = 2048
BATCH = 1
SEQ = 8192
DEPTH = 1
DEC_BATCH = 128
DEC_SEQ = 4
PAST_LEN = 2048
PAGE_SIZE = 128

HEAD_DIM = 128
A_HEADS = 6
A_WIDTH = A_HEADS * HEAD_DIM
IDX_HEADS = 16
IDX_DIM = 64
TOPK_MAX = 256
Q_BLOCK = 128
R_HEADS = 6
R_DK = 128
R_DV = 128
R_WIDTH = R_HEADS * R_DV
RET_CHUNK = 128
ROPE_BASE = 10000.0
MEM_TOKENS = 256
M_HEADS = 4
M_WIDTH = M_HEADS * HEAD_DIM
N_BRANCH = 3
W_IN_COLS = 3 * A_WIDTH + IDX_HEADS * IDX_DIM + IDX_DIM + IDX_HEADS + 2 * R_HEADS * R_DK + 2 * R_WIDTH + M_WIDTH + N_BRANCH * D_MODEL
N_GROUPS = 4
EXP_PER_GROUP = 8
N_EXPERTS = N_GROUPS * EXP_PER_GROUP
EXP_FF = 256
TOP_K_EXP = 2
RMS_EPS = 1e-6

kernel_name = 'hybrid_dsa_retention_memory_hmoe_step'

F32 = jnp.float32


def _rmsnorm(x, g):
    xf = x.astype(F32)
    y = xf * lax.rsqrt(jnp.mean(xf * xf, axis=-1, keepdims=True) + RMS_EPS)
    return (y * g.astype(F32)).astype(x.dtype)


def _rope(x, pos):
    half = x.shape[-1] // 2
    freqs = ROPE_BASE ** (-jnp.arange(half, dtype=F32) / half)
    ang = pos.astype(F32)[:, None] * freqs[None, :]
    cos = jnp.cos(ang)[None, :, None, :]
    sin = jnp.sin(ang)[None, :, None, :]
    xf = x.astype(F32)
    x1, x2 = xf[..., :half], xf[..., half:]
    return jnp.concatenate([x1 * cos - x2 * sin, x1 * sin + x2 * cos], axis=-1).astype(x.dtype)


def _in_splits():
    sizes = (A_WIDTH, A_WIDTH, A_WIDTH, IDX_HEADS * IDX_DIM, IDX_DIM, IDX_HEADS,
             R_HEADS * R_DK, R_HEADS * R_DK, R_WIDTH, R_WIDTH, M_WIDTH, N_BRANCH * D_MODEL)
    return [int(s) for s in np.cumsum(sizes)[:-1]]


def _rows(a, idx):
    return jax.vmap(lambda ab, ib: ab[ib])(a, idx)


def _project(h, pos, w_in, q_norm_a, k_norm_a, idx_k_norm, q_norm_m):
    B, T, _ = h.shape
    qa, ka, va, qi, ki, wi, qr, kr, vr, gr, qm, gates = jnp.split(h @ w_in, _in_splits(), axis=-1)
    qa = _rmsnorm(qa.reshape(B, T, A_HEADS, HEAD_DIM), q_norm_a)
    ka = _rmsnorm(ka.reshape(B, T, A_HEADS, HEAD_DIM), k_norm_a)
    va = va.reshape(B, T, A_HEADS, HEAD_DIM)
    qi = qi.reshape(B, T, IDX_HEADS, IDX_DIM)
    ki = _rmsnorm(ki, idx_k_norm)
    wi = wi * (IDX_HEADS * IDX_DIM) ** -0.5
    qr = _rope(qr.reshape(B, T, R_HEADS, R_DK), pos)
    kr = _rope(kr.reshape(B, T, R_HEADS, R_DK), pos) * R_DK ** -0.5
    vr = vr.reshape(B, T, R_HEADS, R_DV)
    qm = _rmsnorm(qm.reshape(B, T, M_HEADS, HEAD_DIM), q_norm_m)
    gates = jax.nn.sigmoid(gates.reshape(B, T, N_BRANCH, D_MODEL))
    return qa, ka, va, qi, ki, wi, qr, kr, vr, gr, qm, gates


def _dsa_block(q, qi, wi, tpos, ki_all, gather_kv, topk):
    L = ki_all.shape[1]
    rel = jax.nn.relu(jnp.einsum('bqhe,ble->bqhl', qi.astype(F32), ki_all.astype(F32)))
    score = jnp.einsum('bqh,bqhl->bql', wi.astype(F32), rel)
    visible = jnp.arange(L)[None, None, :] <= tpos[None, :, None]
    score = jnp.where(visible, score, -jnp.inf)
    _, idx = lax.top_k(score, topk)
    kg, vg = gather_kv(idx)
    logits = jnp.einsum('bqhd,bqkhd->bqhk', q.astype(F32), kg.astype(F32)) * HEAD_DIM ** -0.5
    valid = (idx <= tpos[None, :, None])[:, :, None, :]
    p = jax.nn.softmax(jnp.where(valid, logits, -jnp.inf), axis=-1)
    return jnp.einsum('bqhk,bqkhd->bqhd', p, vg.astype(F32)).astype(q.dtype)


def _dsa_prompt(qa, ka, va, qi, ki, wi):
    B, T = qa.shape[:2]
    nb = T // Q_BLOCK
    topk = min(TOPK_MAX, T // 4)

    def blocks(a):
        return a.reshape(B, nb, Q_BLOCK, *a.shape[2:]).swapaxes(0, 1)

    def gather_kv(idx):
        return _rows(ka, idx), _rows(va, idx)

    def body(xs):
        q, qib, wib, blk = xs
        tpos = blk * Q_BLOCK + jnp.arange(Q_BLOCK)
        return _dsa_block(q, qib, wib, tpos, ki, gather_kv, topk)

    o = lax.map(body, (blocks(qa), blocks(qi), blocks(wi), jnp.arange(nb)))
    return o.swapaxes(0, 1).reshape(B, T, A_WIDTH)


def _dsa_sample(qa, ka, va, qi, ki, wi, cache_k, cache_v, cache_idx_k, page_table):
    Bd, Td = qa.shape[:2]
    n_pages = PAST_LEN // PAGE_SIZE
    ki_past = cache_idx_k[page_table].reshape(Bd, n_pages * PAGE_SIZE, IDX_DIM).astype(ki.dtype)
    ki_all = jnp.concatenate([ki_past, ki], axis=1)
    topk = min(TOPK_MAX, (PAST_LEN + Td) // 4)

    def gather_kv(idx):
        is_new = (idx >= PAST_LEN)[..., None, None]
        ip = jnp.minimum(idx, PAST_LEN - 1)
        phys = _rows(page_table, ip // PAGE_SIZE)
        off = ip % PAGE_SIZE
        inew = jnp.clip(idx - PAST_LEN, 0, Td - 1)
        kg = jnp.where(is_new, _rows(ka, inew), cache_k[phys, off].astype(ka.dtype))
        vg = jnp.where(is_new, _rows(va, inew), cache_v[phys, off].astype(va.dtype))
        return kg, vg

    def body(xs):
        q, qit, wit, t = xs
        tpos = (PAST_LEN + t)[None]
        return _dsa_block(q[:, None], qit[:, None], wit[:, None], tpos, ki_all, gather_kv, topk)[:, 0]

    o = lax.map(body, (qa.swapaxes(0, 1), qi.swapaxes(0, 1), wi.swapaxes(0, 1), jnp.arange(Td)))
    return o.swapaxes(0, 1).reshape(Bd, Td, A_WIDTH)


def _log_gamma():
    return jnp.log1p(-jnp.exp2(-5.0 - jnp.arange(R_HEADS, dtype=F32)))


def _ret_chunk(q, k, v, s0):
    C = q.shape[1]
    lg = _log_gamma()
    n = jnp.arange(C, dtype=F32)
    diff = n[:, None] - n[None, :]
    dmat = jnp.where(diff >= 0, jnp.exp(lg[:, None, None] * jnp.maximum(diff, 0.0)[None]), 0.0)
    qf, kf, vf = q.astype(F32), k.astype(F32), v.astype(F32)
    att = jnp.einsum('bnhd,bmhd->bhnm', qf, kf) * dmat[None]
    intra = jnp.einsum('bhnm,bmhe->bnhe', att, vf)
    cross = jnp.einsum('bnhd,bhde->bnhe', qf, s0) * jnp.exp(lg[None, :] * (n[:, None] + 1.0))[None, :, :, None]
    kd = kf * jnp.exp(lg[None, :] * (C - 1.0 - n[:, None]))[None, :, :, None]
    s_new = jnp.exp(lg * C)[None, :, None, None] * s0 + jnp.einsum('bmhd,bmhe->bhde', kd, vf)
    return intra + cross, s_new


def _retention_prompt(q, k, v):
    B, T = q.shape[:2]
    nc = T // RET_CHUNK

    def chunks(a):
        return a.reshape(B, nc, RET_CHUNK, *a.shape[2:]).swapaxes(0, 1)

    def step(s, qkv):
        o, s = _ret_chunk(qkv[0], qkv[1], qkv[2], s)
        return s, o

    s, o = lax.scan(step, jnp.zeros((B, R_HEADS, R_DK, R_DV), F32), (chunks(q), chunks(k), chunks(v)))
    return o.swapaxes(0, 1).reshape(B, T, R_HEADS, R_DV), s


def _ret_out(o, g):
    B, T = o.shape[:2]
    o = o * lax.rsqrt(jnp.mean(o * o, axis=-1, keepdims=True) + RMS_EPS)
    return (o.reshape(B, T, R_WIDTH) * jax.nn.silu(g.astype(F32))).astype(g.dtype)


def _mem_kv(mem, mem_norm, w_mem_kv, k_norm_m):
    B, M, _ = mem.shape
    k, v = jnp.split(_rmsnorm(mem, mem_norm) @ w_mem_kv, 2, axis=-1)
    k = _rmsnorm(k.reshape(B, M, M_HEADS, HEAD_DIM), k_norm_m)
    return k, v.reshape(B, M, M_HEADS, HEAD_DIM)


def _mem_attend(q, k, v):
    B, T = q.shape[:2]
    logits = jnp.einsum('bthd,bmhd->bhtm', q.astype(F32), k.astype(F32)) * HEAD_DIM ** -0.5
    p = jax.nn.softmax(logits, axis=-1)
    return jnp.einsum('bhtm,bmhd->bthd', p, v.astype(F32)).astype(q.dtype).reshape(B, T, M_WIDTH)


def _moe(h, w_group, w_router, w_gate_e, w_up_e, w_down_e):
    B, T, D = h.shape
    hf = h.reshape(B * T, D)
    pg = jax.nn.softmax((hf @ w_group).astype(F32), axis=-1)
    g_sel = jnp.argmax(pg, axis=-1)
    p_sel = jnp.max(pg, axis=-1)
    el = (hf @ w_router).astype(F32).reshape(B * T, N_GROUPS, EXP_PER_GROUP)
    el = jnp.take_along_axis(el, g_sel[:, None, None], axis=1)[:, 0]
    top_v, top_i = lax.top_k(el, TOP_K_EXP)
    w = jax.nn.softmax(top_v, axis=-1) * p_sel[:, None]
    e_idx = g_sel[:, None] * EXP_PER_GROUP + top_i
    combine = jnp.sum(jax.nn.one_hot(e_idx, N_EXPERTS, dtype=F32) * w[..., None], axis=1)
    a = jnp.einsum('nd,edf->nef', hf, w_gate_e)
    u = jnp.einsum('nd,edf->nef', hf, w_up_e)
    act = jax.nn.silu(a) * u * combine[:, :, None].astype(h.dtype)
    return jnp.einsum('nef,efd->nd', act, w_down_e).reshape(B, T, D)


def _finish(x, o_a, o_b, o_m, gates, w_br_a, w_br_b, w_br_m, w_out,
            norm2, w_group, w_router, w_gate_e, w_up_e, w_down_e):
    merged = (gates[:, :, 0] * (o_a @ w_br_a) + gates[:, :, 1] * (o_b @ w_br_b)
              + gates[:, :, 2] * (o_m @ w_br_m))
    x = x + merged @ w_out
    return x + _moe(_rmsnorm(x, norm2), w_group, w_router, w_gate_e, w_up_e, w_down_e)


def setup_inputs(seed: int = 0) -> dict:
    key = jax.random.key(seed)
    ks = iter(jax.random.split(key, 40))

    def nrm(shape, scale=1.0):
        return jax.random.normal(next(ks), shape, F32) * scale

    def gain(shape):
        return 1.0 + nrm(shape, 0.02)

    n_pages = PAST_LEN // PAGE_SIZE
    used = DEC_BATCH * n_pages
    n_pool = used + (used + 3) // 4
    page_table = jax.random.permutation(next(ks), n_pool)[:used].astype(jnp.int32).reshape(DEC_BATCH, n_pages)
    Ld = DEPTH
    return {
        'x_prompt': nrm((BATCH, SEQ, D_MODEL)),
        'x_sample': nrm((DEC_BATCH, DEC_SEQ, D_MODEL)),
        'mem_prompt': nrm((BATCH, MEM_TOKENS, D_MODEL)),
        'cache_k': nrm((Ld, n_pool, PAGE_SIZE, A_HEADS, HEAD_DIM)),
        'cache_v': nrm((Ld, n_pool, PAGE_SIZE, A_HEADS, HEAD_DIM)),
        'cache_idx_k': nrm((Ld, n_pool, PAGE_SIZE, IDX_DIM)),
        'state_ret': nrm((Ld, DEC_BATCH, R_HEADS, R_DK, R_DV)),
        'cache_mem_k': nrm((Ld, DEC_BATCH, MEM_TOKENS, M_HEADS, HEAD_DIM)),
        'cache_mem_v': nrm((Ld, DEC_BATCH, MEM_TOKENS, M_HEADS, HEAD_DIM)),
        'page_table': page_table,
        'norm1': gain((Ld, D_MODEL)),
        'w_in': nrm((Ld, D_MODEL, W_IN_COLS), D_MODEL ** -0.5),
        'q_norm_a': gain((Ld, HEAD_DIM)),
        'k_norm_a': gain((Ld, HEAD_DIM)),
        'idx_k_norm': gain((Ld, IDX_DIM)),
        'q_norm_m': gain((Ld, HEAD_DIM)),
        'k_norm_m': gain((Ld, HEAD_DIM)),
        'mem_norm': gain((Ld, D_MODEL)),
        'w_mem_kv': nrm((Ld, D_MODEL, 2 * M_WIDTH), D_MODEL ** -0.5),
        'w_br_a': nrm((Ld, A_WIDTH, D_MODEL), A_WIDTH ** -0.5),
        'w_br_b': nrm((Ld, R_WIDTH, D_MODEL), R_WIDTH ** -0.5),
        'w_br_m': nrm((Ld, M_WIDTH, D_MODEL), M_WIDTH ** -0.5),
        'w_out': nrm((Ld, D_MODEL, D_MODEL), D_MODEL ** -0.5),
        'norm2': gain((Ld, D_MODEL)),
        'w_group': nrm((Ld, D_MODEL, N_GROUPS), D_MODEL ** -0.5),
        'w_router': nrm((Ld, D_MODEL, N_EXPERTS), D_MODEL ** -0.5),
        'w_gate_e': nrm((Ld, N_EXPERTS, D_MODEL, EXP_FF), D_MODEL ** -0.5),
        'w_up_e': nrm((Ld, N_EXPERTS, D_MODEL, EXP_FF), D_MODEL ** -0.5),
        'w_down_e': nrm((Ld, N_EXPERTS, EXP_FF, D_MODEL), EXP_FF ** -0.5),
    }


def reference(x_prompt, x_sample, mem_prompt, cache_k, cache_v, cache_idx_k, state_ret, cache_mem_k, cache_mem_v,
              page_table, norm1, w_in, q_norm_a, k_norm_a, idx_k_norm, q_norm_m, k_norm_m, mem_norm, w_mem_kv,
              w_br_a, w_br_b, w_br_m, w_out, norm2, w_group, w_router, w_gate_e, w_up_e, w_down_e):
    y_p, y_s = x_prompt, x_sample
    pos_p = jnp.arange(x_prompt.shape[1])
    pos_s = PAST_LEN + jnp.arange(x_sample.shape[1])
    kp, vp, ikp, rp, mkp, mvp, ks_, vs_, iks, rs = [], [], [], [], [], [], [], [], [], []
    for l in range(DEPTH):
        proj_w = (w_in[l], q_norm_a[l], k_norm_a[l], idx_k_norm[l], q_norm_m[l])
        out_w = (w_br_a[l], w_br_b[l], w_br_m[l], w_out[l])
        ffn_w = (norm2[l], w_group[l], w_router[l], w_gate_e[l], w_up_e[l], w_down_e[l])
        qa, ka, va, qi, ki, wi, qr, kr, vr, gr, qm, gates = _project(_rmsnorm(y_p, norm1[l]), pos_p, *proj_w)
        o_a = _dsa_prompt(qa, ka, va, qi, ki, wi)
        o_r, s_fin = _retention_prompt(qr, kr, vr)
        mk, mv = _mem_kv(mem_prompt, mem_norm[l], w_mem_kv[l], k_norm_m[l])
        y_p = _finish(y_p, o_a, _ret_out(o_r, gr), _mem_attend(qm, mk, mv), gates, *out_w, *ffn_w)
        kp.append(ka); vp.append(va); ikp.append(ki); rp.append(s_fin.astype(x_prompt.dtype))
        mkp.append(mk); mvp.append(mv)
        qa, ka, va, qi, ki, wi, qr, kr, vr, gr, qm, gates = _project(_rmsnorm(y_s, norm1[l]), pos_s, *proj_w)
        o_a = _dsa_sample(qa, ka, va, qi, ki, wi, cache_k[l], cache_v[l], cache_idx_k[l], page_table)
        o_r, s_new = _ret_chunk(qr, kr, vr, state_ret[l].astype(F32))
        y_s = _finish(y_s, o_a, _ret_out(o_r, gr), _mem_attend(qm, cache_mem_k[l], cache_mem_v[l]), gates,
                      *out_w, *ffn_w)
        ks_.append(ka); vs_.append(va); iks.append(ki); rs.append(s_new.astype(state_ret.dtype))
    return (y_p, y_s, jnp.stack(kp), jnp.stack(vp), jnp.stack(ikp), jnp.stack(rp), jnp.stack(mkp), jnp.stack(mvp),
            jnp.stack(ks_), jnp.stack(vs_), jnp.stack(iks), jnp.stack(rs))
```

```python
import functools

import jax
import jax.numpy as jnp
import numpy as np
from jax import lax
from jax.experimental import pallas as pl
from jax.experimental.pallas import tpu as pltpu

F32 = jnp.float32
BF16 = jnp.bfloat16
I32 = jnp.int32

D_MODEL = 2048
PAST_LEN = 2048
PAGE_SIZE = 128
N_PAGES = PAST_LEN // PAGE_SIZE
HEAD_DIM = 128
A_HEADS = 6
A_WIDTH = A_HEADS * HEAD_DIM
IDX_HEADS = 16
IDX_DIM = 64
TOPK_MAX = 256
R_HEADS = 6
R_DK = 128
R_WIDTH = R_HEADS * 128
RET_CHUNK = 128
ROPE_BASE = 10000.0
M_HEADS = 4
M_WIDTH = M_HEADS * HEAD_DIM
N_BRANCH = 3
N_GROUPS = 4
EXP_PER_GROUP = 8
N_EXPERTS = N_GROUPS * EXP_PER_GROUP
EXP_FF = 256
RMS_EPS = 1e-6

LANES = 128
INT_MIN = -(2 ** 31)
NEG_BIG = -1e30
V7X_VMEM_LIMIT = 56 * 1024 * 1024

_SIZES = (A_WIDTH, A_WIDTH, A_WIDTH, IDX_HEADS * IDX_DIM, IDX_DIM, IDX_HEADS,
          R_HEADS * R_DK, R_HEADS * R_DK, R_WIDTH, R_WIDTH, M_WIDTH, N_BRANCH * D_MODEL)
_OFFS = tuple(int(v) for v in np.cumsum((0,) + _SIZES))
(O_QA, O_KA, O_VA, O_QI, O_KI, O_WI, O_QR, O_KR, O_VR, O_GR, O_QM, O_GATES, O_END) = _OFFS


def _params(n_grid, vmem=V7X_VMEM_LIMIT):
    return pltpu.CompilerParams(dimension_semantics=("arbitrary",) * n_grid, vmem_limit_bytes=vmem)


def _nt_dot(a, b):
    return lax.dot_general(a, b, (((1,), (1,)), ((), ())), preferred_element_type=F32)


def _dot(a, b):
    return jnp.dot(a, b, preferred_element_type=F32)


def _rmsnorm_kernel(x_ref, g_ref, o_ref):
    x = x_ref[...]
    ms = jnp.mean(x * x, axis=-1, keepdims=True)
    o_ref[...] = (x * lax.rsqrt(ms + RMS_EPS) * g_ref[...]).astype(o_ref.dtype)


def _rmsnorm_cast(x, g, tm):
    m, d = x.shape
    return pl.pallas_call(
        _rmsnorm_kernel,
        grid=(m // tm,),
        in_specs=[pl.BlockSpec((tm, d), lambda i: (i, 0)), pl.BlockSpec((1, d), lambda i: (0, 0))],
        out_specs=pl.BlockSpec((tm, d), lambda i: (i, 0)),
        out_shape=jax.ShapeDtypeStruct((m, d), BF16),
        compiler_params=_params(1),
        name="rmsnorm_cast",
    )(x, g.reshape(1, d))


def _proj_plain_kernel(h_ref, w_ref, *o_refs):
    y = _dot(h_ref[...], w_ref[...])
    for o in o_refs:
        o[...] = y.astype(o.dtype)


def _proj_headnorm_kernel(h_ref, w_ref, g_ref, *o_refs, scale):
    y = _dot(h_ref[...], w_ref[...])
    g = g_ref[...]
    for hh in range(y.shape[1] // HEAD_DIM):
        sl = slice(hh * HEAD_DIM, (hh + 1) * HEAD_DIM)
        yh = y[:, sl]
        ms = jnp.mean(yh * yh, axis=-1, keepdims=True)
        z = yh * lax.rsqrt(ms + RMS_EPS) * g
        if scale != 1.0:
            z = z * scale
        for o in o_refs:
            o[:, sl] = z.astype(o.dtype)


def _proj_rope_kernel(h_ref, w_ref, cos_ref, sin_ref, o_ref, *, scale):
    y = _dot(h_ref[...], w_ref[...])
    c = cos_ref[...]
    s = sin_ref[...]
    for hh in range(y.shape[1] // R_DK):
        sl = slice(hh * R_DK, (hh + 1) * R_DK)
        yh = y[:, sl]
        z = yh * c + pltpu.roll(yh, R_DK // 2, 1) * s
        if scale != 1.0:
            z = z * scale
        o_ref[:, sl] = z


def _proj_kw_kernel(h_ref, w_ref, g_ref, kw_ref, kte_ref, kto_ref):
    y = _dot(h_ref[...], w_ref[...])
    lane = lax.broadcasted_iota(I32, y.shape, 1)
    is_k = lane < IDX_DIM
    ms = jnp.sum(jnp.where(is_k, y * y, 0.0), axis=-1, keepdims=True) * (1.0 / IDX_DIM)
    kn = y * lax.rsqrt(ms + RMS_EPS) * g_ref[...]
    kw_ref[...] = jnp.where(is_k, kn, y * (IDX_HEADS * IDX_DIM) ** -0.5)
    e = jnp.where(is_k, kn, 0.0)
    kte_ref[...] = e.T.astype(BF16)
    kto_ref[...] = pltpu.roll(e, IDX_DIM, 1).T.astype(BF16)


def _proj(kern, h, w, extras, extra_specs, outs, tm, tn, name):
    m, k = h.shape
    n = w.shape[1]
    out_shape, out_specs = [], []
    for dt, transposed in outs:
        if transposed:
            out_shape.append(jax.ShapeDtypeStruct((n, m), dt))
            out_specs.append(pl.BlockSpec((tn, tm), lambda j, i: (j, i)))
        else:
            out_shape.append(jax.ShapeDtypeStruct((m, n), dt))
            out_specs.append(pl.BlockSpec((tm, tn), lambda j, i: (i, j)))
    res = pl.pallas_call(
        kern,
        grid=(n // tn, m // tm),
        in_specs=[pl.BlockSpec((tm, k), lambda j, i: (i, 0)),
                  pl.BlockSpec((k, tn), lambda j, i: (0, j))] + list(extra_specs),
        out_specs=out_specs,
        out_shape=out_shape,
        compiler_params=_params(2),
        name=name,
    )(h, w, *extras)
    return res


def _row_spec(tm, width):
    return pl.BlockSpec((tm, width), lambda j, i: (i, 0))


def _const_spec(shape):
    return pl.BlockSpec(shape, lambda j, i: (0,) * len(shape))


def _score_key(score):
    bits = lax.bitcast_convert_type(score, I32)
    return jnp.where(bits < 0, bits ^ jnp.int32(0x7FFFFFFF), bits)


def _dsa_prompt_kernel(qa_ref, qi_ref, kw_ref, kte_ref, kto_ref, k_ref, v_ref, o_ref,
                       keys_ref, wb_ref, *, topk, tq, tk):
    i = pl.program_id(0)
    n_chunks = (i * tq + tq + tk - 1) // tk
    kw = kw_ref[...]
    for hh in range(IDX_HEADS):
        wb_ref[hh] = jnp.broadcast_to(kw[:, IDX_DIM + hh:IDX_DIM + hh + 1], (tq, LANES))
    row = i * tq + lax.broadcasted_iota(I32, (tq, tk), 0)
    reps = tk // LANES

    def wide(a):
        return jnp.concatenate([a] * reps, axis=1)

    def score_chunk(c, carry):
        off = pl.multiple_of(c * tk, tk)
        kte = kte_ref[:, pl.ds(off, tk)]
        kto = kto_ref[:, pl.ds(off, tk)]
        acc = jnp.zeros((tq, tk), F32)
        for p in range(IDX_HEADS // 2):
            lhs = qi_ref[:, p * LANES:(p + 1) * LANES]
            acc = acc + wide(wb_ref[2 * p]) * jnp.maximum(_dot(lhs, kte), 0.0)
            acc = acc + wide(wb_ref[2 * p + 1]) * jnp.maximum(_dot(lhs, kto), 0.0)
        col = off + lax.broadcasted_iota(I32, (tq, tk), 1)
        keys_ref[:, pl.ds(off, tk)] = jnp.where(col <= row, _score_key(acc), INT_MIN)
        return carry

    lax.fori_loop(0, n_chunks, score_chunk, 0)

    def bit_body(it, ua):
        cand_u = ua | jnp.left_shift(jnp.int32(1), 31 - it)
        cand = wide(cand_u ^ INT_MIN)

        def count_chunk(c, acc):
            off = pl.multiple_of(c * tk, tk)
            return acc + jnp.where(keys_ref[:, pl.ds(off, tk)] >= cand, 1.0, 0.0)

        cnt = lax.fori_loop(0, n_chunks, count_chunk, jnp.zeros((tq, tk), F32))
        total = jnp.sum(cnt, axis=1, keepdims=True)
        return jnp.where(total >= topk, cand_u, ua)

    ua = lax.fori_loop(0, 32, bit_body, jnp.zeros((tq, LANES), I32))
    thr = wide(jnp.maximum(ua ^ INT_MIN, INT_MIN + 1))

    for hh in range(A_HEADS):
        sl = slice(hh * HEAD_DIM, (hh + 1) * HEAD_DIM)
        q = qa_ref[:, sl]

        def att_chunk(c, carry, sl=sl, q=q):
            m, l, acc = carry
            off = pl.multiple_of(c * tk, tk)
            s = _nt_dot(q, k_ref[pl.ds(off, tk), sl])
            s = jnp.where(keys_ref[:, pl.ds(off, tk)] >= thr, s, NEG_BIG)
            m_new = jnp.maximum(m, jnp.max(s, axis=1, keepdims=True))
            alpha = jnp.exp(m - m_new)
            p = jnp.exp(s - m_new)
            l = alpha * l + jnp.sum(p, axis=1, keepdims=True)
            acc = alpha * acc + _dot(p.astype(BF16), v_ref[pl.ds(off, tk), sl])
            return m_new, l, acc

        init = (jnp.full((tq, 1), NEG_BIG, F32), jnp.zeros((tq, 1), F32), jnp.zeros((tq, HEAD_DIM), F32))
        m, l, acc = lax.fori_loop(0, n_chunks, att_chunk, init)
        o_ref[:, sl] = (acc / l).astype(o_ref.dtype)


def _dsa_prompt(qa, qi, kw, kte, kto, kb, vb):
    t = qa.shape[0]
    tq, tk = 128, 256
    topk = min(TOPK_MAX, t // 4)
    one = pl.Buffered(1)
    return pl.pallas_call(
        functools.partial(_dsa_prompt_kernel, topk=topk, tq=tq, tk=tk),
        grid=(t // tq,),
        in_specs=[pl.BlockSpec((tq, A_WIDTH), lambda i: (i, 0)),
                  pl.BlockSpec((tq, IDX_HEADS * IDX_DIM), lambda i: (i, 0)),
                  pl.BlockSpec((tq, LANES), lambda i: (i, 0)),
                  pl.BlockSpec((LANES, t), lambda i: (0, 0), pipeline_mode=one),
                  pl.BlockSpec((LANES, t), lambda i: (0, 0), pipeline_mode=one),
                  pl.BlockSpec((t, A_WIDTH), lambda i: (0, 0), pipeline_mode=one),
                  pl.BlockSpec((t, A_WIDTH), lambda i: (0, 0), pipeline_mode=one)],
        out_specs=pl.BlockSpec((tq, A_WIDTH), lambda i: (i, 0)),
        out_shape=jax.ShapeDtypeStruct((t, A_WIDTH), BF16),
        scratch_shapes=[pltpu.VMEM((tq, t), I32), pltpu.VMEM((IDX_HEADS, tq, LANES), F32)],
        compiler_params=_params(1),
        name="dsa_prompt",
    )(qa, qi, kw, kte, kto, kb, vb)


def _diag_rows(acc, n_tok, width):
    sub = lax.broadcasted_iota(I32, (8, width), 0)
    lane_head = lax.broadcasted_iota(I32, (8, width), 1) // HEAD_DIM
    out = jnp.zeros((8, width), F32)
    for t in range(n_tok):
        blk = acc[t * 8:(t + 1) * 8, :]
        r = jnp.sum(jnp.where(sub == lane_head, blk, 0.0), axis=0, keepdims=True)
        out = out + jnp.where(sub == t, jnp.broadcast_to(r, (8, width)), 0.0)
    return out


def _dsa_sample_kernel(pt_ref, a_ref, w_ref, q_ref, *refs, topk, n_tok):
    n_pg = N_PAGES
    idx_pages = refs[:n_pg]
    ki_new = refs[n_pg]
    k_pages = refs[n_pg + 1:2 * n_pg + 1]
    k_new = refs[2 * n_pg + 1]
    v_pages = refs[2 * n_pg + 2:3 * n_pg + 2]
    v_new = refs[3 * n_pg + 2]
    o_ref = refs[3 * n_pg + 3]
    keys8_ref, keys32_ref, lg_ref = refs[3 * n_pg + 4:]
    n_ch = n_pg + 1
    a = a_ref[...]
    wcol = jnp.broadcast_to(w_ref[...], (n_tok * IDX_HEADS, LANES))
    sub8 = lax.broadcasted_iota(I32, (8, LANES), 0)
    lane8 = lax.broadcasted_iota(I32, (8, LANES), 1)

    for j in range(n_ch):
        kp = idx_pages[j][...].astype(BF16) if j < n_pg else ki_new[...]
        r = jnp.maximum(_nt_dot(a, kp), 0.0) * wcol
        k8 = jnp.full((8, LANES), INT_MIN, I32)
        k32 = []
        for t in range(n_tok):
            st = jnp.sum(r[t * IDX_HEADS:(t + 1) * IDX_HEADS, :], axis=0, keepdims=True)
            kt = jnp.broadcast_to(_score_key(st), (8, LANES))
            if j == n_pg:
                kt = jnp.where(lane8 <= t, kt, INT_MIN)
            k8 = jnp.where(sub8 == t, kt, k8)
            k32.append(kt)
        keys8_ref[:, j * LANES:(j + 1) * LANES] = k8
        keys32_ref[:, j * LANES:(j + 1) * LANES] = jnp.concatenate(k32, axis=0)

    def bit_body(it, ua):
        cand_u = ua | jnp.left_shift(jnp.int32(1), 31 - it)
        cand = cand_u ^ INT_MIN
        cnt = jnp.zeros((8, LANES), F32)
        for j in range(n_ch):
            cnt = cnt + jnp.where(keys8_ref[:, j * LANES:(j + 1) * LANES] >= cand, 1.0, 0.0)
        total = jnp.sum(cnt, axis=1, keepdims=True)
        return jnp.where(total >= topk, cand_u, ua)

    ua = lax.fori_loop(0, 32, bit_body, jnp.zeros((8, LANES), I32))
    thr8 = jnp.maximum(ua ^ INT_MIN, INT_MIN + 1)
    thr32 = jnp.concatenate([jnp.broadcast_to(thr8[t:t + 1, :], (8, LANES)) for t in range(n_tok)], axis=0)

    q = q_ref[...]
    for j in range(n_ch):
        kb = k_pages[j][...].astype(BF16) if j < n_pg else k_new[...]
        s = _nt_dot(q, kb)
        sel = keys32_ref[:, j * LANES:(j + 1) * LANES] >= thr32
        lg_ref[:, j * LANES:(j + 1) * LANES] = jnp.where(sel, s, NEG_BIG)
    lg = lg_ref[...]
    m = jnp.max(lg, axis=1, keepdims=True)
    p = jnp.exp(lg - m)
    pn = (p / jnp.sum(p, axis=1, keepdims=True)).astype(BF16)
    acc = jnp.zeros((n_tok * 8, A_WIDTH), F32)
    for j in range(n_ch):
        vb = v_pages[j][...].astype(BF16) if j < n_pg else v_new[...]
        acc = acc + _dot(pn[:, j * LANES:(j + 1) * LANES], vb)
    o_ref[...] = _diag_rows(acc, n_tok, A_WIDTH)


def _dsa_sample(page_table, a, wcol, qblk, idx_pool, ki_new, k_pool, k_new, v_pool, v_new, n_tok):
    nb = a.shape[0]
    topk = min(TOPK_MAX, (PAST_LEN + n_tok) // 4)

    def page_spec(width, j):
        return pl.BlockSpec((None, PAGE_SIZE, width), lambda b, pt, j=j: (pt[b, j], 0, 0))

    def batch_spec(rows, width):
        return pl.BlockSpec((None, rows, width), lambda b, pt: (b, 0, 0))

    in_specs = [batch_spec(n_tok * IDX_HEADS, IDX_DIM), batch_spec(n_tok * IDX_HEADS, 1),
                batch_spec(n_tok * 8, A_WIDTH)]
    in_specs += [page_spec(IDX_DIM, j) for j in range(N_PAGES)] + [batch_spec(PAGE_SIZE, IDX_DIM)]
    in_specs += [page_spec(A_WIDTH, j) for j in range(N_PAGES)] + [batch_spec(PAGE_SIZE, A_WIDTH)]
    in_specs += [page_spec(A_WIDTH, j) for j in range(N_PAGES)] + [batch_spec(PAGE_SIZE, A_WIDTH)]
    n_keys = (N_PAGES + 1) * LANES
    grid_spec = pltpu.PrefetchScalarGridSpec(
        num_scalar_prefetch=1, grid=(nb,), in_specs=in_specs,
        out_specs=batch_spec(8, A_WIDTH),
        scratch_shapes=[pltpu.VMEM((8, n_keys), I32), pltpu.VMEM((n_tok * 8, n_keys), I32),
                        pltpu.VMEM((n_tok * 8, n_keys), F32)])
    return pl.pallas_call(
        functools.partial(_dsa_sample_kernel, topk=topk, n_tok=n_tok),
        grid_spec=grid_spec,
        out_shape=jax.ShapeDtypeStruct((nb, 8, A_WIDTH), F32),
        compiler_params=_params(1),
        name="dsa_sample",
    )(page_table, a, wcol, qblk, *([idx_pool] * N_PAGES), ki_new, *([k_pool] * N_PAGES), k_new,
      *([v_pool] * N_PAGES), v_new)


def _ret_consts(chunk):
    lg = jnp.log1p(-jnp.exp2(-5.0 - jnp.arange(R_HEADS, dtype=F32)))
    n = jnp.arange(chunk, dtype=F32)
    diff = n[:, None] - n[None, :]
    dmat = jnp.where(diff >= 0, jnp.exp(lg[:, None, None] * jnp.maximum(diff, 0.0)[None]), 0.0)
    cdec = jnp.exp(lg[:, None] * (n[None, :] + 1.0))
    kdec = jnp.exp(lg[:, None] * (chunk - 1.0 - n[None, :]))
    sdec = jnp.exp(lg * chunk)
    return dmat, cdec, kdec, sdec


def _ret_out_store(o, g, o_ref, sl):
    on = o * lax.rsqrt(jnp.mean(o * o, axis=-1, keepdims=True) + RMS_EPS)
    o_ref[:, sl] = (on * (g * jax.nn.sigmoid(g))).astype(o_ref.dtype)


def _ret_prompt_kernel(q_ref, k_ref, v_ref, g_ref, dmat_ref, cdec_ref, kdec_ref, sdec_ref,
                       o_ref, sout_ref, s_ref):
    c = pl.program_id(0)

    @pl.when(c == 0)
    def _():
        s_ref[...] = jnp.zeros_like(s_ref)

    for hh in range(R_HEADS):
        sl = slice(hh * R_DK, (hh + 1) * R_DK)
        k = k_ref[:, sl]
        qb = q_ref[:, sl].astype(BF16)
        vb = v_ref[:, sl].astype(BF16)
        att = _nt_dot(qb, k.astype(BF16)) * dmat_ref[hh]
        intra = _dot(att.astype(BF16), vb)
        s0 = s_ref[hh]
        cross = _dot(qb, s0.astype(BF16)) * cdec_ref[hh]
        kd_t = (k * kdec_ref[hh]).T.astype(BF16)
        s_ref[hh] = sdec_ref[hh] * s0 + _dot(kd_t, vb)
        _ret_out_store(intra + cross, g_ref[:, sl], o_ref, sl)

    @pl.when(c == pl.num_programs(0) - 1)
    def _():
        sout_ref[...] = s_ref[...]


def _retention_prompt(qr, kr, vg):
    t = qr.shape[0]
    c = RET_CHUNK
    dmat, cdec, kdec, sdec = _ret_consts(c)
    rep = lambda a: jnp.broadcast_to(a[:, :, None], (R_HEADS, c, LANES))
    cdec_b, kdec_b = rep(cdec), rep(kdec)
    sdec_b = jnp.broadcast_to(sdec[:, None, None], (R_HEADS, R_DK, LANES))
    blk = lambda col: pl.BlockSpec((c, R_WIDTH), lambda i, col=col: (i, col))
    const = lambda: pl.BlockSpec((R_HEADS, c, LANES), lambda i: (0, 0, 0))
    return pl.pallas_call(
        _ret_prompt_kernel,
        grid=(t // c,),
        in_specs=[blk(0), blk(0), blk(0), blk(1), const(), const(), const(), const()],
        out_specs=[pl.BlockSpec((c, R_WIDTH), lambda i: (i, 0)),
                   pl.BlockSpec((R_HEADS, R_DK, LANES), lambda i: (0, 0, 0))],
        out_shape=[jax.ShapeDtypeStruct((t, R_WIDTH), BF16),
                   jax.ShapeDtypeStruct((R_HEADS, R_DK, 128), F32)],
        scratch_shapes=[pltpu.VMEM((R_HEADS, R_DK, 128), F32)],
        compiler_params=_params(1),
        name="retention_prompt",
    )(qr, kr, vg, vg, dmat, cdec_b, kdec_b, sdec_b)


def _ret_sample_kernel(q_ref, k_ref, v_ref, g_ref, s0_ref, dmat_ref, cdec_ref, kdec_ref, sdec_ref,
                       o_ref, sout_ref, kpad_ref, vpad_ref):
    @pl.when(pl.program_id(0) == 0)
    def _():
        kpad_ref[...] = jnp.zeros_like(kpad_ref)
        vpad_ref[...] = jnp.zeros_like(vpad_ref)

    kpad_ref[0:8, :] = k_ref[...]
    vpad_ref[0:8, :] = v_ref[...]
    for hh in range(R_HEADS):
        sl = slice(hh * R_DK, (hh + 1) * R_DK)
        kp = kpad_ref[:, sl]
        vb = vpad_ref[:, sl].astype(BF16)
        qb = q_ref[:, sl].astype(BF16)
        att = _nt_dot(qb, kp.astype(BF16)) * dmat_ref[hh]
        intra = _dot(att.astype(BF16), vb)
        s0 = s0_ref[hh]
        cross = _dot(qb, s0.astype(BF16)) * cdec_ref[hh]
        kd_t = (kp * kdec_ref[hh]).T.astype(BF16)
        sout_ref[hh] = sdec_ref[hh] * s0 + _dot(kd_t, vb)
        _ret_out_store(intra + cross, g_ref[:, sl], o_ref, sl)


def _retention_sample(qr, kr, vr, gr, state, n_tok):
    nb = qr.shape[0]
    dmat, cdec, kdec, sdec = _ret_consts(n_tok)
    dmat_p = jnp.zeros((R_HEADS, 8, LANES), F32).at[:, :n_tok, :n_tok].set(dmat)
    cdec_p = jnp.zeros((R_HEADS, 8, LANES), F32).at[:, :n_tok, :].set(
        jnp.broadcast_to(cdec[:, :, None], (R_HEADS, n_tok, LANES)))
    kdec_p = jnp.zeros((R_HEADS, LANES, LANES), F32).at[:, :n_tok, :].set(
        jnp.broadcast_to(kdec[:, :, None], (R_HEADS, n_tok, LANES)))
    sdec_b = jnp.broadcast_to(sdec[:, None, None], (R_HEADS, R_DK, LANES))
    tok = lambda: pl.BlockSpec((None, 8, R_WIDTH), lambda b: (b, 0, 0))
    st = lambda: pl.BlockSpec((None, R_HEADS, R_DK, 128), lambda b: (b, 0, 0, 0))
    const = lambda rows: pl.BlockSpec((R_HEADS, rows, LANES), lambda b: (0, 0, 0))
    return pl.pallas_call(
        _ret_sample_kernel,
        grid=(nb,),
        in_specs=[tok(), tok(), tok(), tok(), st(), const(8), const(8), const(LANES), const(R_DK)],
        out_specs=[tok(), st()],
        out_shape=[jax.ShapeDtypeStruct((nb, 8, R_WIDTH), F32),
                   jax.ShapeDtypeStruct(state.shape, F32)],
        scratch_shapes=[pltpu.VMEM((LANES, R_WIDTH), F32), pltpu.VMEM((LANES, R_WIDTH), F32)],
        compiler_params=_params(1),
        name="retention_sample",
    )(qr, kr, vr, gr, state, dmat_p, cdec_p, kdec_p, sdec_b)


def _mem_attn_prompt_kernel(q_ref, k_ref, v_ref, o_ref):
    for hh in range(M_HEADS):
        sl = slice(hh * HEAD_DIM, (hh + 1) * HEAD_DIM)
        s = _nt_dot(q_ref[:, sl], k_ref[:, sl])
        p = jnp.exp(s - jnp.max(s, axis=1, keepdims=True))
        pn = (p / jnp.sum(p, axis=1, keepdims=True)).astype(BF16)
        o_ref[:, sl] = _dot(pn, v_ref[:, sl]).astype(o_ref.dtype)


def _mem_attn_prompt(qm, mk, mv, tm):
    t = qm.shape[0]
    nm = mk.shape[0]
    return pl.pallas_call(
        _mem_attn_prompt_kernel,
        grid=(t // tm,),
        in_specs=[pl.BlockSpec((tm, M_WIDTH), lambda i: (i, 0)),
                  pl.BlockSpec((nm, M_WIDTH), lambda i: (0, 0)),
                  pl.BlockSpec((nm, M_WIDTH), lambda i: (0, 0))],
        out_specs=pl.BlockSpec((tm, M_WIDTH), lambda i: (i, 0)),
        out_shape=jax.ShapeDtypeStruct((t, M_WIDTH), BF16),
        compiler_params=_params(1),
        name="mem_attn_prompt",
    )(qm, mk, mv)


def _mem_attn_sample_kernel(q_ref, k_ref, v_ref, o_ref, *, n_tok):
    s = _nt_dot(q_ref[...], k_ref[...].astype(BF16))
    p = jnp.exp(s - jnp.max(s, axis=1, keepdims=True))
    pn = (p / jnp.sum(p, axis=1, keepdims=True)).astype(BF16)
    acc = _dot(pn, v_ref[...].astype(BF16))
    o_ref[...] = _diag_rows(acc, n_tok, M_WIDTH)


def _mem_attn_sample(qblk, mem_k, mem_v, n_tok):
    nb, nm = mem_k.shape[0], mem_k.shape[1]
    return pl.pallas_call(
        functools.partial(_mem_attn_sample_kernel, n_tok=n_tok),
        grid=(nb,),
        in_specs=[pl.BlockSpec((None, n_tok * 8, M_WIDTH), lambda b: (b, 0, 0)),
                  pl.BlockSpec((None, nm, M_WIDTH), lambda b: (b, 0, 0)),
                  pl.BlockSpec((None, nm, M_WIDTH), lambda b: (b, 0, 0))],
        out_specs=pl.BlockSpec((None, 8, M_WIDTH), lambda b: (b, 0, 0)),
        out_shape=jax.ShapeDtypeStruct((nb, 8, M_WIDTH), F32),
        compiler_params=_params(1),
        name="mem_attn_sample",
    )(qblk, mem_k, mem_v)


def _merged_kernel(h_ref, oa_ref, ob_ref, om_ref, wg0_ref, wg1_ref, wg2_ref, wa_ref, wb_ref, wm_ref, o_ref):
    h = h_ref[...]
    acc = None
    for wg, o, wbr in ((wg0_ref, oa_ref, wa_ref), (wg1_ref, ob_ref, wb_ref), (wg2_ref, om_ref, wm_ref)):
        term = jax.nn.sigmoid(_dot(h, wg[...])) * _dot(o[...].astype(BF16), wbr[...])
        acc = term if acc is None else acc + term
    o_ref[...] = acc.astype(o_ref.dtype)


def _merged(h, o_a, o_b, o_m, w_gates, w_a, w_b, w_m, tm, tn):
    m = h.shape[0]
    nj = D_MODEL // tn
    row = lambda width: pl.BlockSpec((tm, width), lambda j, i: (i, 0))
    gate = lambda b: pl.BlockSpec((D_MODEL, tn), lambda j, i, b=b: (0, b * nj + j))
    br = lambda width: pl.BlockSpec((width, tn), lambda j, i: (0, j))
    return pl.pallas_call(
        _merged_kernel,
        grid=(nj, m // tm),
        in_specs=[row(D_MODEL), row(A_WIDTH), row(R_WIDTH), row(M_WIDTH), gate(0), gate(1), gate(2),
                  br(A_WIDTH), br(R_WIDTH), br(M_WIDTH)],
        out_specs=pl.BlockSpec((tm, tn), lambda j, i: (i, j)),
        out_shape=jax.ShapeDtypeStruct((m, D_MODEL), BF16),
        compiler_params=_params(2),
        name="merged",
    )(h, o_a, o_b, o_m, w_gates, w_gates, w_gates, w_a, w_b, w_m)


def _outproj_kernel(x_ref, mg_ref, wo_ref, n2_ref, wrt_ref, x2_ref, h2_ref, rl_ref):
    x2 = x_ref[...] + _dot(mg_ref[...], wo_ref[...])
    x2_ref[...] = x2
    h2 = x2 * lax.rsqrt(jnp.mean(x2 * x2, axis=-1, keepdims=True) + RMS_EPS) * n2_ref[...]
    h2_ref[...] = h2.astype(BF16)
    rl_ref[...] = jnp.dot(h2, wrt_ref[...], preferred_element_type=F32, precision=lax.Precision.HIGHEST)


def _outproj(x, merged, w_out, norm2, w_rt, tm):
    m = x.shape[0]
    one = pl.Buffered(1)
    row = lambda width: pl.BlockSpec((tm, width), lambda i: (i, 0))
    return pl.pallas_call(
        _outproj_kernel,
        grid=(m // tm,),
        in_specs=[row(D_MODEL), row(D_MODEL),
                  pl.BlockSpec((D_MODEL, D_MODEL), lambda i: (0, 0), pipeline_mode=one),
                  pl.BlockSpec((1, D_MODEL), lambda i: (0, 0)),
                  pl.BlockSpec((D_MODEL, LANES), lambda i: (0, 0), pipeline_mode=one)],
        out_specs=[row(D_MODEL), row(D_MODEL), row(LANES)],
        out_shape=[jax.ShapeDtypeStruct((m, D_MODEL), F32), jax.ShapeDtypeStruct((m, D_MODEL), BF16),
                   jax.ShapeDtypeStruct((m, LANES), F32)],
        compiler_params=_params(1),
        name="outproj",
    )(x, merged, w_out, norm2.reshape(1, D_MODEL), w_rt)


def _router_kernel(rl_ref, comb_ref):
    z = rl_ref[...]
    lane = lax.broadcasted_iota(I32, z.shape, 1).astype(F32)
    ninf = -jnp.inf
    big = 1e9
    is_g = lane < N_GROUPS
    gl = jnp.where(is_g, z, ninf)
    gmax = jnp.max(gl, axis=1, keepdims=True)
    gsum = jnp.sum(jnp.where(is_g, jnp.exp(gl - gmax), 0.0), axis=1, keepdims=True)
    p_sel = 1.0 / gsum
    g_sel = jnp.min(jnp.where(gl == gmax, lane, big), axis=1, keepdims=True)
    lo = N_GROUPS + EXP_PER_GROUP * g_sel
    el = jnp.where(lane >= lo, jnp.where(lane < lo + EXP_PER_GROUP, z, ninf), ninf)
    v1 = jnp.max(el, axis=1, keepdims=True)
    i1 = jnp.min(jnp.where(el == v1, lane, big), axis=1, keepdims=True)
    el2 = jnp.where(lane == i1, ninf, el)
    v2 = jnp.max(el2, axis=1, keepdims=True)
    i2 = jnp.min(jnp.where(el2 == v2, lane, big), axis=1, keepdims=True)
    e2 = jnp.exp(v2 - v1)
    w1 = 1.0 / (1.0 + e2)
    w2 = e2 / (1.0 + e2)
    comb_ref[...] = jnp.where(lane == i1, w1 * p_sel, 0.0) + jnp.where(lane == i2, w2 * p_sel, 0.0)


def _router(rl, tm):
    m = rl.shape[0]
    return pl.pallas_call(
        _router_kernel,
        grid=(m // tm,),
        in_specs=[pl.BlockSpec((tm, LANES), lambda i: (i, 0))],
        out_specs=pl.BlockSpec((tm, LANES), lambda i: (i, 0)),
        out_shape=jax.ShapeDtypeStruct((m, LANES), F32),
        compiler_params=_params(1),
        name="router",
    )(rl)


def _moe_kernel(x2_ref, h2_ref, comb_ref, wg_ref, wu_ref, wd_ref, y_ref, acc_ref, *, ne):
    k = pl.program_id(1)

    @pl.when(k == 0)
    def _():
        acc_ref[...] = x2_ref[...]

    h2 = h2_ref[...]
    comb = comb_ref[...]
    lane = lax.broadcasted_iota(I32, comb.shape, 1)
    for e in range(ne):
        ce = jnp.sum(jnp.where(lane == N_GROUPS + k * ne + e, comb, 0.0), axis=1, keepdims=True)
        a = _dot(h2, wg_ref[e])
        u = _dot(h2, wu_ref[e])
        act = (a * jax.nn.sigmoid(a)) * u * ce
        acc_ref[...] += _dot(act.astype(BF16), wd_ref[e])

    @pl.when(k == pl.num_programs(1) - 1)
    def _():
        y_ref[...] = acc_ref[...]


def _moe(x2, h2, comb, wg, wu, wd, tm, ne):
    m = x2.shape[0]
    row = lambda width: pl.BlockSpec((tm, width), lambda i, k: (i, 0))
    return pl.pallas_call(
        functools.partial(_moe_kernel, ne=ne),
        grid=(m // tm, N_EXPERTS // ne),
        in_specs=[row(D_MODEL), row(D_MODEL), row(LANES),
                  pl.BlockSpec((ne, D_MODEL, EXP_FF), lambda i, k: (k, 0, 0)),
                  pl.BlockSpec((ne, D_MODEL, EXP_FF), lambda i, k: (k, 0, 0)),
                  pl.BlockSpec((ne, EXP_FF, D_MODEL), lambda i, k: (k, 0, 0))],
        out_specs=row(D_MODEL),
        out_shape=jax.ShapeDtypeStruct((m, D_MODEL), F32),
        scratch_shapes=[pltpu.VMEM((tm, D_MODEL), F32)],
        compiler_params=_params(2),
        name="moe",
    )(x2, h2, comb, wg, wu, wd)


def _rope_tables(pos):
    half = R_DK // 2
    freqs = ROPE_BASE ** (-jnp.arange(half, dtype=F32) / half)
    ang = pos.astype(F32)[:, None] * freqs[None, :]
    cos, sin = jnp.cos(ang), jnp.sin(ang)
    return jnp.concatenate([cos, cos], axis=1), jnp.concatenate([-sin, sin], axis=1)


def _layer_weights(l, w_in, w_mem_kv, w_br_a, w_br_b, w_br_m, w_out, w_group, w_router,
                   w_gate_e, w_up_e, w_down_e):
    wi = w_in[l]
    col = lambda a, b: wi[:, a:b].astype(BF16)
    w_kw = jnp.pad(wi[:, O_KI:O_QR], ((0, 0), (0, LANES - (O_QR - O_KI)))).astype(BF16)
    w_rt = jnp.pad(jnp.concatenate([w_group[l], w_router[l]], axis=1),
                   ((0, 0), (0, LANES - N_GROUPS - N_EXPERTS)))
    return dict(
        qa=col(O_QA, O_KA), ka=col(O_KA, O_VA), va=col(O_VA, O_QI), qi=col(O_QI, O_KI), kw=w_kw,
        qr=col(O_QR, O_KR), kr=col(O_KR, O_VR), vg=col(O_VR, O_QM), qm=col(O_QM, O_GATES),
        gates=col(O_GATES, O_END),
        mk=w_mem_kv[l][:, :M_WIDTH].astype(BF16), mv=w_mem_kv[l][:, M_WIDTH:].astype(BF16),
        br_a=w_br_a[l].astype(BF16), br_b=w_br_b[l].astype(BF16), br_m=w_br_m[l].astype(BF16),
        out=w_out[l].astype(BF16), rt=w_rt,
        ge=w_gate_e[l].astype(BF16), ue=w_up_e[l].astype(BF16), de=w_down_e[l].astype(BF16))


def _project(x2d, pos, w, norms, tm):
    t = x2d.shape[0]
    norm1, q_norm_a, k_norm_a, idx_k_norm, q_norm_m = norms
    h = _rmsnorm_cast(x2d, norm1, tm)
    hd = lambda g: g.reshape(1, HEAD_DIM)
    gspec = _const_spec((1, HEAD_DIM))
    scale = HEAD_DIM ** -0.5
    (qa,) = _proj(functools.partial(_proj_headnorm_kernel, scale=scale), h, w["qa"], [hd(q_norm_a)], [gspec],
                  [(BF16, False)], tm, A_WIDTH, "proj_qa")
    ka, kab = _proj(functools.partial(_proj_headnorm_kernel, scale=1.0), h, w["ka"], [hd(k_norm_a)], [gspec],
                    [(F32, False), (BF16, False)], tm, A_WIDTH, "proj_ka")
    va, vab = _proj(_proj_plain_kernel, h, w["va"], [], [], [(F32, False), (BF16, False)], tm, A_WIDTH, "proj_va")
    (qi,) = _proj(_proj_plain_kernel, h, w["qi"], [], [], [(BF16, False)], tm, 512, "proj_qi")
    g_idx = jnp.pad(idx_k_norm, (0, LANES - IDX_DIM)).reshape(1, LANES)
    kw, kte, kto = _proj(_proj_kw_kernel, h, w["kw"], [g_idx], [_const_spec((1, LANES))],
                         [(F32, False), (BF16, True), (BF16, True)], tm, LANES, "proj_kw")
    cos2, sin2 = _rope_tables(pos)
    tabs, tab_specs = [cos2, sin2], [_row_spec(tm, R_DK), _row_spec(tm, R_DK)]
    (qr,) = _proj(functools.partial(_proj_rope_kernel, scale=1.0), h, w["qr"], tabs, tab_specs,
                  [(F32, False)], tm, R_WIDTH, "proj_qr")
    (kr,) = _proj(functools.partial(_proj_rope_kernel, scale=R_DK ** -0.5), h, w["kr"], tabs, tab_specs,
                  [(F32, False)], tm, R_WIDTH, "proj_kr")
    (vg,) = _proj(_proj_plain_kernel, h, w["vg"], [], [], [(F32, False)], tm, R_WIDTH, "proj_vg")
    (qm,) = _proj(functools.partial(_proj_headnorm_kernel, scale=scale), h, w["qm"], [hd(q_norm_m)], [gspec],
                  [(BF16, False)], tm, M_WIDTH, "proj_qm")
    return dict(h=h, qa=qa, ka=ka, kab=kab, va=va, vab=vab, qi=qi, kw=kw, kte=kte, kto=kto,
                qr=qr, kr=kr, vg=vg, qm=qm)


def _finish(x2d, h, o_a, o_b, o_m, w, norm2, tm):
    merged = _merged(h, o_a, o_b, o_m, w["gates"], w["br_a"], w["br_b"], w["br_m"], tm, 512)
    x2, h2, rl = _outproj(x2d, merged, w["out"], norm2, w["rt"], min(tm, 256))
    comb = _router(rl, tm)
    return _moe(x2, h2, comb, w["ge"], w["ue"], w["de"], tm, 2)


def _block_diag_q(q, n_heads):
    nb, n_tok, width = q.shape
    slot = jnp.arange(8)[:, None]
    head = (jnp.arange(width) // HEAD_DIM)[None, :]
    mask = (slot == head).astype(q.dtype)
    return (q[:, :, None, :] * mask[None, None]).reshape(nb, n_tok * 8, width)


def kernel(x_prompt, x_sample, mem_prompt, cache_k, cache_v, cache_idx_k, state_ret, cache_mem_k, cache_mem_v,
           page_table, norm1, w_in, q_norm_a, k_norm_a, idx_k_norm, q_norm_m, k_norm_m, mem_norm, w_mem_kv,
           w_br_a, w_br_b, w_br_m, w_out, norm2, w_group, w_router, w_gate_e, w_up_e, w_down_e):
    depth = w_in.shape[0]
    bp, seq, _ = x_prompt.shape
    nb, n_tok, _ = x_sample.shape
    assert bp == 1 and n_tok <= 8
    y_p = x_prompt.reshape(seq, D_MODEL)
    y_s = x_sample.reshape(nb * n_tok, D_MODEL)
    pos_p = jnp.arange(seq)
    pos_s = jnp.tile(PAST_LEN + jnp.arange(n_tok), nb)
    outs = [[] for _ in range(10)]
    for l in range(depth):
        w = _layer_weights(l, w_in, w_mem_kv, w_br_a, w_br_b, w_br_m, w_out, w_group, w_router,
                           w_gate_e, w_up_e, w_down_e)
        norms = (norm1[l], q_norm_a[l], k_norm_a[l], idx_k_norm[l], q_norm_m[l])

        tm = 512
        p = _project(y_p, pos_p, w, norms, tm)
        o_a = _dsa_prompt(p["qa"], p["qi"], p["kw"], p["kte"], p["kto"], p["kab"], p["vab"])
        o_r, s_fin = _retention_prompt(p["qr"], p["kr"], p["vg"])
        n_mem = mem_prompt.shape[1]
        hm = _rmsnorm_cast(mem_prompt.reshape(n_mem, D_MODEL), mem_norm[l], n_mem)
        mk, mkb = _proj(functools.partial(_proj_headnorm_kernel, scale=1.0), hm, w["mk"],
                        [k_norm_m[l].reshape(1, HEAD_DIM)], [_const_spec((1, HEAD_DIM))],
                        [(F32, False), (BF16, False)], n_mem, M_WIDTH, "proj_mk")
        mv, mvb = _proj(_proj_plain_kernel, hm, w["mv"], [], [], [(F32, False), (BF16, False)],
                        n_mem, M_WIDTH, "proj_mv")
        o_m = _mem_attn_prompt(p["qm"], mkb, mvb, tm)
        y_p = _finish(y_p, p["h"], o_a, o_r, o_m, w, norm2[l], tm)
        outs[0].append(p["ka"].reshape(bp, seq, A_HEADS, HEAD_DIM))
        outs[1].append(p["va"].reshape(bp, seq, A_HEADS, HEAD_DIM))
        outs[2].append(p["kw"][:, :IDX_DIM].reshape(bp, seq, IDX_DIM))
        outs[3].append(s_fin.reshape(bp, R_HEADS, R_DK, 128))
        outs[4].append(mk.reshape(bp, n_mem, M_HEADS, HEAD_DIM))
        outs[5].append(mv.reshape(bp, n_mem, M_HEADS, HEAD_DIM))

        ts = nb * n_tok
        s = _project(y_s, pos_s, w, norms, ts)
        pad_tok = lambda a: jnp.pad(a.reshape(nb, n_tok, -1), ((0, 0), (0, 8 - n_tok), (0, 0)))
        pad_page = lambda a: jnp.pad(a.reshape(nb, n_tok, -1), ((0, 0), (0, PAGE_SIZE - n_tok), (0, 0)))
        a_idx = s["qi"].reshape(nb, n_tok * IDX_HEADS, IDX_DIM)
        w_idx = s["kw"][:, IDX_DIM:IDX_DIM + IDX_HEADS].reshape(nb, n_tok * IDX_HEADS, 1)
        qblk = _block_diag_q(s["qa"].reshape(nb, n_tok, A_WIDTH), A_HEADS)
        ki_new = pad_page(s["kw"][:, :IDX_DIM].astype(BF16))
        n_pool = cache_k.shape[1]
        o_a_s = _dsa_sample(page_table, a_idx, w_idx, qblk,
                            cache_idx_k[l], ki_new,
                            cache_k[l].reshape(n_pool, PAGE_SIZE, A_WIDTH), pad_page(s["kab"]),
                            cache_v[l].reshape(n_pool, PAGE_SIZE, A_WIDTH), pad_page(s["vab"]), n_tok)
        o_r_s, s_new = _retention_sample(pad_tok(s["qr"]), pad_tok(s["kr"]), pad_tok(s["vg"][:, :R_WIDTH]),
                                         pad_tok(s["vg"][:, R_WIDTH:]), state_ret[l], n_tok)
        qblk_m = _block_diag_q(s["qm"].reshape(nb, n_tok, M_WIDTH), M_HEADS)
        n_mem_s = cache_mem_k.shape[2]
        o_m_s = _mem_attn_sample(qblk_m, cache_mem_k[l].reshape(nb, n_mem_s, M_WIDTH),
                                 cache_mem_v[l].reshape(nb, n_mem_s, M_WIDTH), n_tok)
        unpad = lambda a: a[:, :n_tok, :].reshape(ts, -1)
        y_s = _finish(y_s, s["h"], unpad(o_a_s), unpad(o_r_s), unpad(o_m_s), w, norm2[l], ts)
        outs[6].append(s["ka"].reshape(nb, n_tok, A_HEADS, HEAD_DIM))
        outs[7].append(s["va"].reshape(nb, n_tok, A_HEADS, HEAD_DIM))
        outs[8].append(s["kw"][:, :IDX_DIM].reshape(nb, n_tok, IDX_DIM))
        outs[9].append(s_new)
    stk = [jnp.stack(o) for o in outs]
    return (y_p.reshape(bp, seq, D_MODEL), y_s.reshape(nb, n_tok, D_MODEL), *stk)
```

```python
import functools

import jax
import jax.numpy as jnp
import numpy as np
from jax import lax
from jax.experimental import pallas as pl
from jax.experimental.pallas import tpu as pltpu

F32 = jnp.float32
BF16 = jnp.bfloat16
I32 = jnp.int32

D_MODEL = 2048
PAST_LEN = 2048
PAGE_SIZE = 128
N_PAGES = PAST_LEN // PAGE_SIZE
HEAD_DIM = 128
A_HEADS = 6
A_WIDTH = A_HEADS * HEAD_DIM
IDX_HEADS = 16
IDX_DIM = 64
TOPK_MAX = 256
R_HEADS = 6
R_DK = 128
R_WIDTH = R_HEADS * 128
RET_CHUNK = 128
ROPE_BASE = 10000.0
M_HEADS = 4
M_WIDTH = M_HEADS * HEAD_DIM
N_BRANCH = 3
N_GROUPS = 4
EXP_PER_GROUP = 8
N_EXPERTS = N_GROUPS * EXP_PER_GROUP
EXP_FF = 256
RMS_EPS = 1e-6

LANES = 128
INT_MIN = -(2 ** 31)
NEG_BIG = -1e30
LOG2_E = 1.4426950408889634
V7X_VMEM_LIMIT = 56 * 1024 * 1024

_SIZES = (A_WIDTH, A_WIDTH, A_WIDTH, IDX_HEADS * IDX_DIM, IDX_DIM, IDX_HEADS,
          R_HEADS * R_DK, R_HEADS * R_DK, R_WIDTH, R_WIDTH, M_WIDTH, N_BRANCH * D_MODEL)
_OFFS = tuple(int(v) for v in np.cumsum((0,) + _SIZES))
(O_QA, O_KA, O_VA, O_QI, O_KI, O_WI, O_QR, O_KR, O_VR, O_GR, O_QM, O_GATES, O_END) = _OFFS


def _params(n_grid, vmem=V7X_VMEM_LIMIT):
    return pltpu.CompilerParams(dimension_semantics=("arbitrary",) * n_grid, vmem_limit_bytes=vmem)


def _nt_dot(a, b):
    return lax.dot_general(a, b, (((1,), (1,)), ((), ())), preferred_element_type=F32)


def _dot(a, b):
    return jnp.dot(a, b, preferred_element_type=F32)


def _rmsnorm_kernel(x_ref, g_ref, o_ref):
    x = x_ref[...]
    ms = jnp.mean(x * x, axis=-1, keepdims=True)
    o_ref[...] = (x * lax.rsqrt(ms + RMS_EPS) * g_ref[...]).astype(o_ref.dtype)


def _rmsnorm_cast(x, g, tm):
    m, d = x.shape
    return pl.pallas_call(
        _rmsnorm_kernel,
        grid=(m // tm,),
        in_specs=[pl.BlockSpec((tm, d), lambda i: (i, 0)), pl.BlockSpec((1, d), lambda i: (0, 0))],
        out_specs=pl.BlockSpec((tm, d), lambda i: (i, 0)),
        out_shape=jax.ShapeDtypeStruct((m, d), BF16),
        compiler_params=_params(1),
        name="rmsnorm_cast",
    )(x, g.reshape(1, d))


def _store_cols(o_ref, hh, z):
    if len(o_ref.shape) == 3:
        o_ref[hh] = z.astype(o_ref.dtype)
    else:
        o_ref[:, hh * HEAD_DIM:(hh + 1) * HEAD_DIM] = z.astype(o_ref.dtype)


def _proj_plain_kernel(h_ref, w_ref, *o_refs):
    y = _dot(h_ref[...], w_ref[...])
    for o in o_refs:
        if len(o.shape) == 3:
            for hh in range(o.shape[0]):
                o[hh] = y[:, hh * HEAD_DIM:(hh + 1) * HEAD_DIM].astype(o.dtype)
        else:
            o[...] = y.astype(o.dtype)


def _proj_headnorm_kernel(h_ref, w_ref, g_ref, *o_refs, scale):
    y = _dot(h_ref[...], w_ref[...])
    g = g_ref[...]
    for hh in range(y.shape[1] // HEAD_DIM):
        sl = slice(hh * HEAD_DIM, (hh + 1) * HEAD_DIM)
        yh = y[:, sl]
        ms = jnp.mean(yh * yh, axis=-1, keepdims=True)
        z = yh * lax.rsqrt(ms + RMS_EPS) * g
        if scale != 1.0:
            z = z * scale
        for o in o_refs:
            _store_cols(o, hh, z)


def _proj_rope_kernel(h_ref, w_ref, cos_ref, sin_ref, o_ref, *, scale):
    y = _dot(h_ref[...], w_ref[...])
    c = cos_ref[...]
    s = sin_ref[...]
    for hh in range(y.shape[1] // R_DK):
        sl = slice(hh * R_DK, (hh + 1) * R_DK)
        yh = y[:, sl]
        z = yh * c + pltpu.roll(yh, R_DK // 2, 1) * s
        if scale != 1.0:
            z = z * scale
        o_ref[:, sl] = z


def _proj_kw_kernel(h_ref, w_ref, g_ref, kw_ref, kt_ref, kte_ref, kto_ref):
    y = _dot(h_ref[...], w_ref[...])
    lane = lax.broadcasted_iota(I32, y.shape, 1)
    is_k = lane < IDX_DIM
    ms = jnp.sum(jnp.where(is_k, y * y, 0.0), axis=-1, keepdims=True) * (1.0 / IDX_DIM)
    kn = y * lax.rsqrt(ms + RMS_EPS) * g_ref[...]
    kw_ref[...] = jnp.where(is_k, kn, y * (IDX_HEADS * IDX_DIM) ** -0.5)
    et = jnp.where(is_k, kn, 0.0).T
    kt_ref[...] = et
    kte_ref[...] = et.astype(BF16)
    kto_ref[...] = pltpu.roll(et, IDX_DIM, 0).astype(BF16)


ROW, TRANSPOSED, HEAD_MAJOR = "row", "transposed", "head_major"


def _proj(kern, h, w, extras, extra_specs, outs, tm, tn, name):
    m, k = h.shape
    n = w.shape[1]
    out_shape, out_specs = [], []
    for dt, layout in outs:
        if layout == TRANSPOSED:
            out_shape.append(jax.ShapeDtypeStruct((n, m), dt))
            out_specs.append(pl.BlockSpec((tn, tm), lambda j, i: (j, i)))
        elif layout == HEAD_MAJOR:
            out_shape.append(jax.ShapeDtypeStruct((n // HEAD_DIM, m, HEAD_DIM), dt))
            out_specs.append(pl.BlockSpec((tn // HEAD_DIM, tm, HEAD_DIM), lambda j, i: (j, i, 0)))
        else:
            out_shape.append(jax.ShapeDtypeStruct((m, n), dt))
            out_specs.append(pl.BlockSpec((tm, tn), lambda j, i: (i, j)))
    res = pl.pallas_call(
        kern,
        grid=(n // tn, m // tm),
        in_specs=[pl.BlockSpec((tm, k), lambda j, i: (i, 0)),
                  pl.BlockSpec((k, tn), lambda j, i: (0, j))] + list(extra_specs),
        out_specs=out_specs,
        out_shape=out_shape,
        compiler_params=_params(2),
        name=name,
    )(h, w, *extras)
    return res


def _row_spec(tm, width):
    return pl.BlockSpec((tm, width), lambda j, i: (i, 0))


def _const_spec(shape):
    return pl.BlockSpec(shape, lambda j, i: (0,) * len(shape))


def _score_key(score):
    bits = lax.bitcast_convert_type(score, I32)
    return jnp.where(bits < 0, bits ^ jnp.int32(0x7FFFFFFF), bits)


def _dsa_prompt_kernel(qa_ref, qi_ref, kw_ref, kte_ref, kto_ref, k_ref, v_ref, o_ref,
                       keys_ref, wb_ref, m_ref, l_ref, acc_ref, a_ref, s_ref, p_ref, *, topk, tq, ts, tk):
    i = pl.program_id(0)
    n_vis = i * tq + tq
    n_sc = (n_vis + ts - 1) // ts
    n_ch = (n_vis + tk - 1) // tk
    kw = kw_ref[...]
    for hh in range(IDX_HEADS):
        wb_ref[hh] = jnp.broadcast_to(kw[:, IDX_DIM + hh:IDX_DIM + hh + 1], (tq, LANES))
    row = i * tq + lax.broadcasted_iota(I32, (tq, LANES), 0)
    lane = lax.broadcasted_iota(I32, (tq, LANES), 1)

    def score_chunk(c, carry):
        off = pl.multiple_of(c * ts, ts)
        kte = kte_ref[:, pl.ds(off, ts)]
        kto = kto_ref[:, pl.ds(off, ts)]
        acc = [jnp.zeros((tq, LANES), F32) for _ in range(ts // LANES)]
        for p in range(IDX_HEADS // 2):
            lhs = qi_ref[:, p * LANES:(p + 1) * LANES]
            for w_head, rhs in ((wb_ref[2 * p], kte), (wb_ref[2 * p + 1], kto)):
                r = jnp.maximum(_dot(lhs, rhs), 0.0)
                for u in range(ts // LANES):
                    acc[u] = acc[u] + w_head * r[:, u * LANES:(u + 1) * LANES]
        for u in range(ts // LANES):
            col = off + u * LANES + lane
            keys_ref[:, pl.ds(off + u * LANES, LANES)] = jnp.where(col <= row, _score_key(acc[u]), INT_MIN)
        return carry

    lax.fori_loop(0, n_sc, score_chunk, 0)

    @pl.when(n_sc * ts < n_ch * tk)
    def _():
        keys_ref[:, pl.ds(pl.multiple_of(n_sc * ts, ts), ts)] = jnp.full((tq, ts), INT_MIN, I32)

    def bit_body(it, ua):
        cand_u = ua | jnp.left_shift(jnp.int32(1), 31 - it)
        cand = cand_u ^ INT_MIN

        def count_chunk(c, cnt):
            off = pl.multiple_of(c * tk, tk)
            for u in range(tk // LANES):
                cnt = cnt + jnp.where(keys_ref[:, pl.ds(off + u * LANES, LANES)] >= cand, 1.0, 0.0)
            return cnt

        cnt = lax.fori_loop(0, n_ch, count_chunk, jnp.zeros((tq, LANES), F32))
        total = jnp.sum(cnt, axis=1, keepdims=True)
        return jnp.where(total >= topk, cand_u, ua)

    ua = lax.fori_loop(0, 32, bit_body, jnp.zeros((tq, LANES), I32))
    thr = jnp.maximum(ua ^ INT_MIN, INT_MIN + 1)

    m_ref[...] = jnp.full(m_ref.shape, NEG_BIG, F32)
    l_ref[...] = jnp.zeros(l_ref.shape, F32)
    acc_ref[...] = jnp.zeros(acc_ref.shape, F32)

    def att_chunk(c, carry):
        off = pl.multiple_of(c * tk, tk)
        n_u = tk // LANES
        bias = jnp.concatenate(
            [jnp.where(keys_ref[:, pl.ds(off + u * LANES, LANES)] >= thr, 0.0, NEG_BIG) for u in range(n_u)], axis=1)
        for hh in range(A_HEADS):
            s_ref[hh] = _nt_dot(qa_ref[:, hh * HEAD_DIM:(hh + 1) * HEAD_DIM], k_ref[hh, pl.ds(off, tk), :]) + bias
        for hh in range(A_HEADS):
            m_old = m_ref[hh]
            mx = s_ref[hh, :, 0:LANES]
            for u in range(1, n_u):
                mx = jnp.maximum(mx, s_ref[hh, :, u * LANES:(u + 1) * LANES])
            m_new = jnp.maximum(m_old, jnp.broadcast_to(jnp.max(mx, axis=1, keepdims=True), (tq, LANES)))
            alpha = jnp.exp2(m_old - m_new)
            psum = jnp.zeros((tq, LANES), F32)
            for u in range(n_u):
                p = jnp.exp2(s_ref[hh, :, u * LANES:(u + 1) * LANES] - m_new)
                psum = psum + p
                p_ref[hh, :, u * LANES:(u + 1) * LANES] = p.astype(BF16)
            l_ref[hh] = alpha * l_ref[hh] + jnp.broadcast_to(jnp.sum(psum, axis=1, keepdims=True), (tq, LANES))
            m_ref[hh] = m_new
            a_ref[hh] = alpha
        for hh in range(A_HEADS):
            acc_ref[hh] = a_ref[hh] * acc_ref[hh] + _dot(p_ref[hh], v_ref[hh, pl.ds(off, tk), :])
        return carry

    lax.fori_loop(0, n_ch, att_chunk, 0)
    for hh in range(A_HEADS):
        o_ref[:, hh * HEAD_DIM:(hh + 1) * HEAD_DIM] = (acc_ref[hh] / l_ref[hh]).astype(o_ref.dtype)


def _dsa_prompt(qa, qi, kw, kte, kto, kb, vb):
    t = qa.shape[0]
    tq, ts, tk = 128, 256, 512
    assert t % tk == 0
    topk = min(TOPK_MAX, t // 4)
    one = pl.Buffered(1)
    head_state = pltpu.VMEM((A_HEADS, tq, LANES), F32)
    return pl.pallas_call(
        functools.partial(_dsa_prompt_kernel, topk=topk, tq=tq, ts=ts, tk=tk),
        grid=(t // tq,),
        in_specs=[pl.BlockSpec((tq, A_WIDTH), lambda i: (i, 0)),
                  pl.BlockSpec((tq, IDX_HEADS * IDX_DIM), lambda i: (i, 0)),
                  pl.BlockSpec((tq, LANES), lambda i: (i, 0)),
                  pl.BlockSpec((LANES, t), lambda i: (0, 0), pipeline_mode=one),
                  pl.BlockSpec((LANES, t), lambda i: (0, 0), pipeline_mode=one),
                  pl.BlockSpec((A_HEADS, t, HEAD_DIM), lambda i: (0, 0, 0), pipeline_mode=one),
                  pl.BlockSpec((A_HEADS, t, HEAD_DIM), lambda i: (0, 0, 0), pipeline_mode=one)],
        out_specs=pl.BlockSpec((tq, A_WIDTH), lambda i: (i, 0)),
        out_shape=jax.ShapeDtypeStruct((t, A_WIDTH), BF16),
        scratch_shapes=[pltpu.VMEM((tq, t), I32), pltpu.VMEM((IDX_HEADS, tq, LANES), F32),
                        head_state, head_state, head_state, head_state,
                        pltpu.VMEM((A_HEADS, tq, tk), F32), pltpu.VMEM((A_HEADS, tq, tk), BF16)],
        compiler_params=_params(1),
        name="dsa_prompt",
    )(qa, qi, kw, kte, kto, kb, vb)


def _diag_rows(acc, n_tok, width):
    sub = lax.broadcasted_iota(I32, (8, width), 0)
    lane_head = lax.broadcasted_iota(I32, (8, width), 1) // HEAD_DIM
    out = jnp.zeros((8, width), F32)
    for t in range(n_tok):
        blk = acc[t * 8:(t + 1) * 8, :]
        r = jnp.sum(jnp.where(sub == lane_head, blk, 0.0), axis=0, keepdims=True)
        out = out + jnp.where(sub == t, jnp.broadcast_to(r, (8, width)), 0.0)
    return out


def _dsa_sample_kernel(pt_ref, a_ref, w_ref, q_ref, *refs, topk, n_tok):
    n_pg = N_PAGES
    idx_pages = refs[:n_pg]
    ki_new = refs[n_pg]
    k_pages = refs[n_pg + 1:2 * n_pg + 1]
    k_new = refs[2 * n_pg + 1]
    v_pages = refs[2 * n_pg + 2:3 * n_pg + 2]
    v_new = refs[3 * n_pg + 2]
    o_ref = refs[3 * n_pg + 3]
    keys8_ref, keys32_ref, lg_ref = refs[3 * n_pg + 4:]
    n_ch = n_pg + 1
    a = a_ref[...]
    wcol = jnp.broadcast_to(w_ref[...], (n_tok * IDX_HEADS, LANES))
    sub8 = lax.broadcasted_iota(I32, (8, LANES), 0)
    lane8 = lax.broadcasted_iota(I32, (8, LANES), 1)

    for j in range(n_ch):
        kp = idx_pages[j][...].astype(BF16) if j < n_pg else ki_new[...]
        r = jnp.maximum(_dot(a, kp), 0.0) * wcol
        k8 = jnp.full((8, LANES), INT_MIN, I32)
        k32 = []
        for t in range(n_tok):
            st = jnp.sum(r[t * IDX_HEADS:(t + 1) * IDX_HEADS, :], axis=0, keepdims=True)
            kt = jnp.broadcast_to(_score_key(st), (8, LANES))
            if j == n_pg:
                kt = jnp.where(lane8 <= t, kt, INT_MIN)
            k8 = jnp.where(sub8 == t, kt, k8)
            k32.append(kt)
        keys8_ref[:, j * LANES:(j + 1) * LANES] = k8
        keys32_ref[:, j * LANES:(j + 1) * LANES] = jnp.concatenate(k32, axis=0)

    def bit_body(it, ua):
        cand_u = ua | jnp.left_shift(jnp.int32(1), 31 - it)
        cand = cand_u ^ INT_MIN
        cnt = jnp.zeros((8, LANES), F32)
        for j in range(n_ch):
            cnt = cnt + jnp.where(keys8_ref[:, j * LANES:(j + 1) * LANES] >= cand, 1.0, 0.0)
        total = jnp.sum(cnt, axis=1, keepdims=True)
        return jnp.where(total >= topk, cand_u, ua)

    ua = lax.fori_loop(0, 32, bit_body, jnp.zeros((8, LANES), I32))
    thr8 = jnp.maximum(ua ^ INT_MIN, INT_MIN + 1)
    thr32 = jnp.concatenate([jnp.broadcast_to(thr8[t:t + 1, :], (8, LANES)) for t in range(n_tok)], axis=0)

    for j in range(n_ch):
        page = k_pages[j] if j < n_pg else k_new
        s = None
        for hh in range(A_HEADS):
            sh = _nt_dot(q_ref[:, hh * HEAD_DIM:(hh + 1) * HEAD_DIM], page[hh].astype(BF16))
            s = sh if s is None else s + sh
        sel = keys32_ref[:, j * LANES:(j + 1) * LANES] >= thr32
        lg_ref[:, j * LANES:(j + 1) * LANES] = jnp.where(sel, s, NEG_BIG)
    lg = lg_ref[...]
    m = jnp.max(lg, axis=1, keepdims=True)
    p = jnp.exp2(lg - m)
    pn = (p / jnp.sum(p, axis=1, keepdims=True)).astype(BF16)
    acc = [jnp.zeros((n_tok * 8, HEAD_DIM), F32) for _ in range(A_HEADS)]
    for j in range(n_ch):
        page = v_pages[j] if j < n_pg else v_new
        pj = pn[:, j * LANES:(j + 1) * LANES]
        for hh in range(A_HEADS):
            acc[hh] = acc[hh] + _dot(pj, page[hh].astype(BF16))
    o_ref[...] = _diag_rows(jnp.concatenate(acc, axis=1), n_tok, A_WIDTH)


def _dsa_sample(page_table, a, wcol, qblk, idx_pool, ki_new, k_pool, k_new, v_pool, v_new, n_tok):
    nb = a.shape[0]
    topk = min(TOPK_MAX, (PAST_LEN + n_tok) // 4)

    def batch_spec(rows, width):
        return pl.BlockSpec((None, rows, width), lambda b, pt: (b, 0, 0))

    idx_page = lambda j: pl.BlockSpec((None, IDX_DIM, PAGE_SIZE), lambda b, pt, j=j: (pt[b, j], 0, 0))
    kv_page = lambda j: pl.BlockSpec((None, A_HEADS, PAGE_SIZE, HEAD_DIM), lambda b, pt, j=j: (pt[b, j], 0, 0, 0))
    kv_new = pl.BlockSpec((None, A_HEADS, PAGE_SIZE, HEAD_DIM), lambda b, pt: (b, 0, 0, 0))
    in_specs = [batch_spec(n_tok * IDX_HEADS, IDX_DIM), batch_spec(n_tok * IDX_HEADS, 1),
                batch_spec(n_tok * 8, A_WIDTH)]
    in_specs += [idx_page(j) for j in range(N_PAGES)] + [batch_spec(IDX_DIM, PAGE_SIZE)]
    in_specs += [kv_page(j) for j in range(N_PAGES)] + [kv_new]
    in_specs += [kv_page(j) for j in range(N_PAGES)] + [kv_new]
    n_keys = (N_PAGES + 1) * LANES
    grid_spec = pltpu.PrefetchScalarGridSpec(
        num_scalar_prefetch=1, grid=(nb,), in_specs=in_specs,
        out_specs=batch_spec(8, A_WIDTH),
        scratch_shapes=[pltpu.VMEM((8, n_keys), I32), pltpu.VMEM((n_tok * 8, n_keys), I32),
                        pltpu.VMEM((n_tok * 8, n_keys), F32)])
    return pl.pallas_call(
        functools.partial(_dsa_sample_kernel, topk=topk, n_tok=n_tok),
        grid_spec=grid_spec,
        out_shape=jax.ShapeDtypeStruct((nb, 8, A_WIDTH), F32),
        compiler_params=_params(1),
        name="dsa_sample",
    )(page_table, a, wcol, qblk, *([idx_pool] * N_PAGES), ki_new, *([k_pool] * N_PAGES), k_new,
      *([v_pool] * N_PAGES), v_new)


def _ret_consts(chunk):
    lg = jnp.log1p(-jnp.exp2(-5.0 - jnp.arange(R_HEADS, dtype=F32)))
    n = jnp.arange(chunk, dtype=F32)
    diff = n[:, None] - n[None, :]
    dmat = jnp.where(diff >= 0, jnp.exp(lg[:, None, None] * jnp.maximum(diff, 0.0)[None]), 0.0)
    cdec = jnp.exp(lg[:, None] * (n[None, :] + 1.0))
    kdec = jnp.exp(lg[:, None] * (chunk - 1.0 - n[None, :]))
    sdec = jnp.exp(lg * chunk)
    return dmat, cdec, kdec, sdec


def _ret_out_store(o, g, o_ref, sl):
    on = o * lax.rsqrt(jnp.mean(o * o, axis=-1, keepdims=True) + RMS_EPS)
    o_ref[:, sl] = (on * (g * jax.nn.sigmoid(g))).astype(o_ref.dtype)


def _ret_prompt_kernel(q_ref, k_ref, v_ref, g_ref, dmat_ref, cdec_ref, kdec_ref, sdec_ref,
                       o_ref, sout_ref, s_ref):
    c = pl.program_id(0)

    @pl.when(c == 0)
    def _():
        s_ref[...] = jnp.zeros_like(s_ref)

    for hh in range(R_HEADS):
        sl = slice(hh * R_DK, (hh + 1) * R_DK)
        k = k_ref[:, sl]
        qb = q_ref[:, sl].astype(BF16)
        vb = v_ref[:, sl].astype(BF16)
        att = _nt_dot(qb, k.astype(BF16)) * dmat_ref[hh]
        intra = _dot(att.astype(BF16), vb)
        s0 = s_ref[hh]
        cross = _dot(qb, s0.astype(BF16)) * cdec_ref[hh]
        kd_t = (k * kdec_ref[hh]).T.astype(BF16)
        s_ref[hh] = sdec_ref[hh] * s0 + _dot(kd_t, vb)
        _ret_out_store(intra + cross, g_ref[:, sl], o_ref, sl)

    @pl.when(c == pl.num_programs(0) - 1)
    def _():
        sout_ref[...] = s_ref[...]


def _retention_prompt(qr, kr, vg):
    t = qr.shape[0]
    c = RET_CHUNK
    dmat, cdec, kdec, sdec = _ret_consts(c)
    rep = lambda a: jnp.broadcast_to(a[:, :, None], (R_HEADS, c, LANES))
    cdec_b, kdec_b = rep(cdec), rep(kdec)
    sdec_b = jnp.broadcast_to(sdec[:, None, None], (R_HEADS, R_DK, LANES))
    blk = lambda col: pl.BlockSpec((c, R_WIDTH), lambda i, col=col: (i, col))
    const = lambda: pl.BlockSpec((R_HEADS, c, LANES), lambda i: (0, 0, 0))
    return pl.pallas_call(
        _ret_prompt_kernel,
        grid=(t // c,),
        in_specs=[blk(0), blk(0), blk(0), blk(1), const(), const(), const(), const()],
        out_specs=[pl.BlockSpec((c, R_WIDTH), lambda i: (i, 0)),
                   pl.BlockSpec((R_HEADS, R_DK, LANES), lambda i: (0, 0, 0))],
        out_shape=[jax.ShapeDtypeStruct((t, R_WIDTH), BF16),
                   jax.ShapeDtypeStruct((R_HEADS, R_DK, 128), F32)],
        scratch_shapes=[pltpu.VMEM((R_HEADS, R_DK, 128), F32)],
        compiler_params=_params(1),
        name="retention_prompt",
    )(qr, kr, vg, vg, dmat, cdec_b, kdec_b, sdec_b)


def _ret_sample_kernel(q_ref, k_ref, v_ref, g_ref, s0_ref, dmat_ref, cdec_ref, kdec_ref, sdec_ref,
                       o_ref, sout_ref, kpad_ref, vpad_ref):
    @pl.when(pl.program_id(0) == 0)
    def _():
        kpad_ref[...] = jnp.zeros_like(kpad_ref)
        vpad_ref[...] = jnp.zeros_like(vpad_ref)

    kpad_ref[0:8, :] = k_ref[...]
    vpad_ref[0:8, :] = v_ref[...]
    for hh in range(R_HEADS):
        sl = slice(hh * R_DK, (hh + 1) * R_DK)
        kp = kpad_ref[:, sl]
        vb = vpad_ref[:, sl].astype(BF16)
        qb = q_ref[:, sl].astype(BF16)
        att = _nt_dot(qb, kp.astype(BF16)) * dmat_ref[hh]
        intra = _dot(att.astype(BF16), vb)
        s0 = s0_ref[hh]
        cross = _dot(qb, s0.astype(BF16)) * cdec_ref[hh]
        kd_t = (kp * kdec_ref[hh]).T.astype(BF16)
        sout_ref[hh] = sdec_ref[hh] * s0 + _dot(kd_t, vb)
        _ret_out_store(intra + cross, g_ref[:, sl], o_ref, sl)


def _retention_sample(qr, kr, vr, gr, state, n_tok):
    nb = qr.shape[0]
    dmat, cdec, kdec, sdec = _ret_consts(n_tok)
    dmat_p = jnp.zeros((R_HEADS, 8, LANES), F32).at[:, :n_tok, :n_tok].set(dmat)
    cdec_p = jnp.zeros((R_HEADS, 8, LANES), F32).at[:, :n_tok, :].set(
        jnp.broadcast_to(cdec[:, :, None], (R_HEADS, n_tok, LANES)))
    kdec_p = jnp.zeros((R_HEADS, LANES, LANES), F32).at[:, :n_tok, :].set(
        jnp.broadcast_to(kdec[:, :, None], (R_HEADS, n_tok, LANES)))
    sdec_b = jnp.broadcast_to(sdec[:, None, None], (R_HEADS, R_DK, LANES))
    tok = lambda: pl.BlockSpec((None, 8, R_WIDTH), lambda b: (b, 0, 0))
    st = lambda: pl.BlockSpec((None, R_HEADS, R_DK, 128), lambda b: (b, 0, 0, 0))
    const = lambda rows: pl.BlockSpec((R_HEADS, rows, LANES), lambda b: (0, 0, 0))
    return pl.pallas_call(
        _ret_sample_kernel,
        grid=(nb,),
        in_specs=[tok(), tok(), tok(), tok(), st(), const(8), const(8), const(LANES), const(R_DK)],
        out_specs=[tok(), st()],
        out_shape=[jax.ShapeDtypeStruct((nb, 8, R_WIDTH), F32),
                   jax.ShapeDtypeStruct(state.shape, F32)],
        scratch_shapes=[pltpu.VMEM((LANES, R_WIDTH), F32), pltpu.VMEM((LANES, R_WIDTH), F32)],
        compiler_params=_params(1),
        name="retention_sample",
    )(qr, kr, vr, gr, state, dmat_p, cdec_p, kdec_p, sdec_b)


def _mem_attn_prompt_kernel(q_ref, k_ref, v_ref, o_ref):
    for hh in range(M_HEADS):
        sl = slice(hh * HEAD_DIM, (hh + 1) * HEAD_DIM)
        s = _nt_dot(q_ref[:, sl], k_ref[:, sl])
        p = jnp.exp2(s - jnp.max(s, axis=1, keepdims=True))
        pn = (p / jnp.sum(p, axis=1, keepdims=True)).astype(BF16)
        o_ref[:, sl] = _dot(pn, v_ref[:, sl]).astype(o_ref.dtype)


def _mem_attn_prompt(qm, mk, mv, tm):
    t = qm.shape[0]
    nm = mk.shape[0]
    return pl.pallas_call(
        _mem_attn_prompt_kernel,
        grid=(t // tm,),
        in_specs=[pl.BlockSpec((tm, M_WIDTH), lambda i: (i, 0)),
                  pl.BlockSpec((nm, M_WIDTH), lambda i: (0, 0)),
                  pl.BlockSpec((nm, M_WIDTH), lambda i: (0, 0))],
        out_specs=pl.BlockSpec((tm, M_WIDTH), lambda i: (i, 0)),
        out_shape=jax.ShapeDtypeStruct((t, M_WIDTH), BF16),
        compiler_params=_params(1),
        name="mem_attn_prompt",
    )(qm, mk, mv)


def _mem_attn_sample_kernel(q_ref, k_ref, v_ref, o_ref, *, n_tok):
    s = None
    for hh in range(M_HEADS):
        sh = _nt_dot(q_ref[:, hh * HEAD_DIM:(hh + 1) * HEAD_DIM], k_ref[:, hh, :].astype(BF16))
        s = sh if s is None else s + sh
    p = jnp.exp2(s - jnp.max(s, axis=1, keepdims=True))
    pn = (p / jnp.sum(p, axis=1, keepdims=True)).astype(BF16)
    acc = jnp.concatenate([_dot(pn, v_ref[:, hh, :].astype(BF16)) for hh in range(M_HEADS)], axis=1)
    o_ref[...] = _diag_rows(acc, n_tok, M_WIDTH)


def _mem_attn_sample(qblk, mem_k, mem_v, n_tok):
    nb, nm = mem_k.shape[0], mem_k.shape[1]
    return pl.pallas_call(
        functools.partial(_mem_attn_sample_kernel, n_tok=n_tok),
        grid=(nb,),
        in_specs=[pl.BlockSpec((None, n_tok * 8, M_WIDTH), lambda b: (b, 0, 0)),
                  pl.BlockSpec((None, nm, M_HEADS, HEAD_DIM), lambda b: (b, 0, 0, 0)),
                  pl.BlockSpec((None, nm, M_HEADS, HEAD_DIM), lambda b: (b, 0, 0, 0))],
        out_specs=pl.BlockSpec((None, 8, M_WIDTH), lambda b: (b, 0, 0)),
        out_shape=jax.ShapeDtypeStruct((nb, 8, M_WIDTH), F32),
        compiler_params=_params(1),
        name="mem_attn_sample",
    )(qblk, mem_k, mem_v)


def _merged_kernel(h_ref, oa_ref, ob_ref, om_ref, wg0_ref, wg1_ref, wg2_ref, wa_ref, wb_ref, wm_ref, o_ref):
    h = h_ref[...]
    acc = None
    for wg, o, wbr in ((wg0_ref, oa_ref, wa_ref), (wg1_ref, ob_ref, wb_ref), (wg2_ref, om_ref, wm_ref)):
        term = jax.nn.sigmoid(_dot(h, wg[...])) * _dot(o[...].astype(BF16), wbr[...])
        acc = term if acc is None else acc + term
    o_ref[...] = acc.astype(o_ref.dtype)


def _merged(h, o_a, o_b, o_m, w_gates, w_a, w_b, w_m, tm, tn):
    m = h.shape[0]
    nj = D_MODEL // tn
    row = lambda width: pl.BlockSpec((tm, width), lambda j, i: (i, 0))
    gate = lambda b: pl.BlockSpec((D_MODEL, tn), lambda j, i, b=b: (0, b * nj + j))
    br = lambda width: pl.BlockSpec((width, tn), lambda j, i: (0, j))
    return pl.pallas_call(
        _merged_kernel,
        grid=(nj, m // tm),
        in_specs=[row(D_MODEL), row(A_WIDTH), row(R_WIDTH), row(M_WIDTH), gate(0), gate(1), gate(2),
                  br(A_WIDTH), br(R_WIDTH), br(M_WIDTH)],
        out_specs=pl.BlockSpec((tm, tn), lambda j, i: (i, j)),
        out_shape=jax.ShapeDtypeStruct((m, D_MODEL), BF16),
        compiler_params=_params(2),
        name="merged",
    )(h, o_a, o_b, o_m, w_gates, w_gates, w_gates, w_a, w_b, w_m)


def _outproj_kernel(x_ref, mg_ref, wo_ref, n2_ref, wrt_ref, x2_ref, h2_ref, rl_ref):
    x2 = x_ref[...] + _dot(mg_ref[...], wo_ref[...])
    x2_ref[...] = x2
    h2 = x2 * lax.rsqrt(jnp.mean(x2 * x2, axis=-1, keepdims=True) + RMS_EPS) * n2_ref[...]
    h2_ref[...] = h2.astype(BF16)
    rl_ref[...] = jnp.dot(h2, wrt_ref[...], preferred_element_type=F32, precision=lax.Precision.HIGHEST)


def _outproj(x, merged, w_out, norm2, w_rt, tm):
    m = x.shape[0]
    one = pl.Buffered(1)
    row = lambda width: pl.BlockSpec((tm, width), lambda i: (i, 0))
    return pl.pallas_call(
        _outproj_kernel,
        grid=(m // tm,),
        in_specs=[row(D_MODEL), row(D_MODEL),
                  pl.BlockSpec((D_MODEL, D_MODEL), lambda i: (0, 0), pipeline_mode=one),
                  pl.BlockSpec((1, D_MODEL), lambda i: (0, 0)),
                  pl.BlockSpec((D_MODEL, LANES), lambda i: (0, 0), pipeline_mode=one)],
        out_specs=[row(D_MODEL), row(D_MODEL), row(LANES)],
        out_shape=[jax.ShapeDtypeStruct((m, D_MODEL), F32), jax.ShapeDtypeStruct((m, D_MODEL), BF16),
                   jax.ShapeDtypeStruct((m, LANES), F32)],
        compiler_params=_params(1),
        name="outproj",
    )(x, merged, w_out, norm2.reshape(1, D_MODEL), w_rt)


def _router_kernel(rl_ref, comb_ref):
    z = rl_ref[...]
    lane = lax.broadcasted_iota(I32, z.shape, 1).astype(F32)
    ninf = -jnp.inf
    big = 1e9
    is_g = lane < N_GROUPS
    gl = jnp.where(is_g, z, ninf)
    gmax = jnp.max(gl, axis=1, keepdims=True)
    gsum = jnp.sum(jnp.where(is_g, jnp.exp(gl - gmax), 0.0), axis=1, keepdims=True)
    p_sel = 1.0 / gsum
    g_sel = jnp.min(jnp.where(gl == gmax, lane, big), axis=1, keepdims=True)
    lo = N_GROUPS + EXP_PER_GROUP * g_sel
    el = jnp.where(lane >= lo, jnp.where(lane < lo + EXP_PER_GROUP, z, ninf), ninf)
    v1 = jnp.max(el, axis=1, keepdims=True)
    i1 = jnp.min(jnp.where(el == v1, lane, big), axis=1, keepdims=True)
    el2 = jnp.where(lane == i1, ninf, el)
    v2 = jnp.max(el2, axis=1, keepdims=True)
    i2 = jnp.min(jnp.where(el2 == v2, lane, big), axis=1, keepdims=True)
    e2 = jnp.exp(v2 - v1)
    w1 = 1.0 / (1.0 + e2)
    w2 = e2 / (1.0 + e2)
    comb_ref[...] = jnp.where(lane == i1, w1 * p_sel, 0.0) + jnp.where(lane == i2, w2 * p_sel, 0.0)


def _router(rl, tm):
    m = rl.shape[0]
    return pl.pallas_call(
        _router_kernel,
        grid=(m // tm,),
        in_specs=[pl.BlockSpec((tm, LANES), lambda i: (i, 0))],
        out_specs=pl.BlockSpec((tm, LANES), lambda i: (i, 0)),
        out_shape=jax.ShapeDtypeStruct((m, LANES), F32),
        compiler_params=_params(1),
        name="router",
    )(rl)


def _moe_kernel(x2_ref, h2_ref, comb_ref, wg_ref, wu_ref, wd_ref, y_ref, acc_ref, *, ne):
    k = pl.program_id(1)

    @pl.when(k == 0)
    def _():
        acc_ref[...] = x2_ref[...]

    h2 = h2_ref[...]
    comb = comb_ref[...]
    lane = lax.broadcasted_iota(I32, comb.shape, 1)
    for e in range(ne):
        ce = jnp.sum(jnp.where(lane == N_GROUPS + k * ne + e, comb, 0.0), axis=1, keepdims=True)
        a = _dot(h2, wg_ref[e])
        u = _dot(h2, wu_ref[e])
        act = (a * jax.nn.sigmoid(a)) * u * ce
        acc_ref[...] += _dot(act.astype(BF16), wd_ref[e])

    @pl.when(k == pl.num_programs(1) - 1)
    def _():
        y_ref[...] = acc_ref[...]


def _moe(x2, h2, comb, wg, wu, wd, tm, ne):
    m = x2.shape[0]
    row = lambda width: pl.BlockSpec((tm, width), lambda i, k: (i, 0))
    return pl.pallas_call(
        functools.partial(_moe_kernel, ne=ne),
        grid=(m // tm, N_EXPERTS // ne),
        in_specs=[row(D_MODEL), row(D_MODEL), row(LANES),
                  pl.BlockSpec((ne, D_MODEL, EXP_FF), lambda i, k: (k, 0, 0)),
                  pl.BlockSpec((ne, D_MODEL, EXP_FF), lambda i, k: (k, 0, 0)),
                  pl.BlockSpec((ne, EXP_FF, D_MODEL), lambda i, k: (k, 0, 0))],
        out_specs=row(D_MODEL),
        out_shape=jax.ShapeDtypeStruct((m, D_MODEL), F32),
        scratch_shapes=[pltpu.VMEM((tm, D_MODEL), F32)],
        compiler_params=_params(2),
        name="moe",
    )(x2, h2, comb, wg, wu, wd)


def _rope_tables(pos):
    half = R_DK // 2
    freqs = ROPE_BASE ** (-jnp.arange(half, dtype=F32) / half)
    ang = pos.astype(F32)[:, None] * freqs[None, :]
    cos, sin = jnp.cos(ang), jnp.sin(ang)
    return jnp.concatenate([cos, cos], axis=1), jnp.concatenate([-sin, sin], axis=1)


def _layer_weights(l, w_in, w_mem_kv, w_br_a, w_br_b, w_br_m, w_out, w_group, w_router,
                   w_gate_e, w_up_e, w_down_e):
    wi = w_in[l]
    col = lambda a, b: wi[:, a:b].astype(BF16)
    w_kw = jnp.pad(wi[:, O_KI:O_QR], ((0, 0), (0, LANES - (O_QR - O_KI)))).astype(BF16)
    w_rt = jnp.pad(jnp.concatenate([w_group[l], w_router[l]], axis=1),
                   ((0, 0), (0, LANES - N_GROUPS - N_EXPERTS)))
    return dict(
        qa=col(O_QA, O_KA), ka=col(O_KA, O_VA), va=col(O_VA, O_QI), qi=col(O_QI, O_KI), kw=w_kw,
        qr=col(O_QR, O_KR), kr=col(O_KR, O_VR), vg=col(O_VR, O_QM), qm=col(O_QM, O_GATES),
        gates=col(O_GATES, O_END),
        mk=w_mem_kv[l][:, :M_WIDTH].astype(BF16), mv=w_mem_kv[l][:, M_WIDTH:].astype(BF16),
        br_a=w_br_a[l].astype(BF16), br_b=w_br_b[l].astype(BF16), br_m=w_br_m[l].astype(BF16),
        out=w_out[l].astype(BF16), rt=w_rt,
        ge=w_gate_e[l].astype(BF16), ue=w_up_e[l].astype(BF16), de=w_down_e[l].astype(BF16))


def _project(x2d, pos, w, norms, tm):
    t = x2d.shape[0]
    norm1, q_norm_a, k_norm_a, idx_k_norm, q_norm_m = norms
    h = _rmsnorm_cast(x2d, norm1, tm)
    hd = lambda g: g.reshape(1, HEAD_DIM)
    gspec = _const_spec((1, HEAD_DIM))
    scale = HEAD_DIM ** -0.5 * LOG2_E
    (qa,) = _proj(functools.partial(_proj_headnorm_kernel, scale=scale), h, w["qa"], [hd(q_norm_a)], [gspec],
                  [(BF16, ROW)], tm, A_WIDTH, "proj_qa")
    ka, kab = _proj(functools.partial(_proj_headnorm_kernel, scale=1.0), h, w["ka"], [hd(k_norm_a)], [gspec],
                    [(F32, HEAD_MAJOR), (BF16, HEAD_MAJOR)], tm, A_WIDTH, "proj_ka")
    va, vab = _proj(_proj_plain_kernel, h, w["va"], [], [], [(F32, HEAD_MAJOR), (BF16, HEAD_MAJOR)], tm, A_WIDTH,
                    "proj_va")
    (qi,) = _proj(_proj_plain_kernel, h, w["qi"], [], [], [(BF16, ROW)], tm, 512, "proj_qi")
    g_idx = jnp.pad(idx_k_norm, (0, LANES - IDX_DIM)).reshape(1, LANES)
    kw, kt, kte, kto = _proj(_proj_kw_kernel, h, w["kw"], [g_idx], [_const_spec((1, LANES))],
                             [(F32, ROW), (F32, TRANSPOSED), (BF16, TRANSPOSED), (BF16, TRANSPOSED)], tm, LANES,
                             "proj_kw")
    cos2, sin2 = _rope_tables(pos)
    tabs, tab_specs = [cos2, sin2], [_row_spec(tm, R_DK), _row_spec(tm, R_DK)]
    (qr,) = _proj(functools.partial(_proj_rope_kernel, scale=1.0), h, w["qr"], tabs, tab_specs,
                  [(F32, ROW)], tm, R_WIDTH, "proj_qr")
    (kr,) = _proj(functools.partial(_proj_rope_kernel, scale=R_DK ** -0.5), h, w["kr"], tabs, tab_specs,
                  [(F32, ROW)], tm, R_WIDTH, "proj_kr")
    (vg,) = _proj(_proj_plain_kernel, h, w["vg"], [], [], [(F32, ROW)], tm, R_WIDTH, "proj_vg")
    (qm,) = _proj(functools.partial(_proj_headnorm_kernel, scale=scale), h, w["qm"], [hd(q_norm_m)], [gspec],
                  [(BF16, ROW)], tm, M_WIDTH, "proj_qm")
    return dict(h=h, qa=qa, ka=ka, kab=kab, va=va, vab=vab, qi=qi, kw=kw, kt=kt, kte=kte, kto=kto,
                qr=qr, kr=kr, vg=vg, qm=qm)


def _finish(x2d, h, o_a, o_b, o_m, w, norm2, tm):
    merged = _merged(h, o_a, o_b, o_m, w["gates"], w["br_a"], w["br_b"], w["br_m"], tm, 512)
    x2, h2, rl = _outproj(x2d, merged, w["out"], norm2, w["rt"], min(tm, 256))
    comb = _router(rl, tm)
    return _moe(x2, h2, comb, w["ge"], w["ue"], w["de"], tm, 2)


def _block_diag_q(q, n_heads):
    nb, n_tok, width = q.shape
    slot = jnp.arange(8)[:, None]
    head = (jnp.arange(width) // HEAD_DIM)[None, :]
    mask = (slot == head).astype(q.dtype)
    return (q[:, :, None, :] * mask[None, None]).reshape(nb, n_tok * 8, width)


def kernel(x_prompt, x_sample, mem_prompt, cache_k, cache_v, cache_idx_k, state_ret, cache_mem_k, cache_mem_v,
           page_table, norm1, w_in, q_norm_a, k_norm_a, idx_k_norm, q_norm_m, k_norm_m, mem_norm, w_mem_kv,
           w_br_a, w_br_b, w_br_m, w_out, norm2, w_group, w_router, w_gate_e, w_up_e, w_down_e):
    depth = w_in.shape[0]
    bp, seq, _ = x_prompt.shape
    nb, n_tok, _ = x_sample.shape
    assert bp == 1 and n_tok <= 8
    y_p = x_prompt.reshape(seq, D_MODEL)
    y_s = x_sample.reshape(nb * n_tok, D_MODEL)
    pos_p = jnp.arange(seq)
    pos_s = jnp.tile(PAST_LEN + jnp.arange(n_tok), nb)
    outs = [[] for _ in range(10)]
    for l in range(depth):
        w = _layer_weights(l, w_in, w_mem_kv, w_br_a, w_br_b, w_br_m, w_out, w_group, w_router,
                           w_gate_e, w_up_e, w_down_e)
        norms = (norm1[l], q_norm_a[l], k_norm_a[l], idx_k_norm[l], q_norm_m[l])

        tm = 512
        p = _project(y_p, pos_p, w, norms, tm)
        o_a = _dsa_prompt(p["qa"], p["qi"], p["kw"], p["kte"], p["kto"], p["kab"], p["vab"])
        o_r, s_fin = _retention_prompt(p["qr"], p["kr"], p["vg"])
        n_mem = mem_prompt.shape[1]
        hm = _rmsnorm_cast(mem_prompt.reshape(n_mem, D_MODEL), mem_norm[l], n_mem)
        mk, mkb = _proj(functools.partial(_proj_headnorm_kernel, scale=1.0), hm, w["mk"],
                        [k_norm_m[l].reshape(1, HEAD_DIM)], [_const_spec((1, HEAD_DIM))],
                        [(F32, ROW), (BF16, ROW)], n_mem, M_WIDTH, "proj_mk")
        mv, mvb = _proj(_proj_plain_kernel, hm, w["mv"], [], [], [(F32, ROW), (BF16, ROW)],
                        n_mem, M_WIDTH, "proj_mv")
        o_m = _mem_attn_prompt(p["qm"], mkb, mvb, tm)
        y_p = _finish(y_p, p["h"], o_a, o_r, o_m, w, norm2[l], tm)
        outs[0].append(jnp.transpose(p["ka"], (1, 0, 2)).reshape(bp, seq, A_HEADS, HEAD_DIM))
        outs[1].append(jnp.transpose(p["va"], (1, 0, 2)).reshape(bp, seq, A_HEADS, HEAD_DIM))
        outs[2].append(p["kt"][:IDX_DIM].T.reshape(bp, seq, IDX_DIM))
        outs[3].append(s_fin.reshape(bp, R_HEADS, R_DK, 128))
        outs[4].append(mk.reshape(bp, n_mem, M_HEADS, HEAD_DIM))
        outs[5].append(mv.reshape(bp, n_mem, M_HEADS, HEAD_DIM))

        ts = nb * n_tok
        s = _project(y_s, pos_s, w, norms, ts)
        pad_tok = lambda a: jnp.pad(a.reshape(nb, n_tok, -1), ((0, 0), (0, 8 - n_tok), (0, 0)))
        a_idx = s["qi"].reshape(nb, n_tok * IDX_HEADS, IDX_DIM)
        w_idx = s["kw"][:, IDX_DIM:IDX_DIM + IDX_HEADS].reshape(nb, n_tok * IDX_HEADS, 1)
        qblk = _block_diag_q(s["qa"].reshape(nb, n_tok, A_WIDTH), A_HEADS)
        ki_new = jnp.pad(jnp.transpose(s["kte"][:IDX_DIM].reshape(IDX_DIM, nb, n_tok), (1, 0, 2)),
                         ((0, 0), (0, 0), (0, PAGE_SIZE - n_tok)))
        new_page = lambda a: jnp.pad(jnp.transpose(a.reshape(A_HEADS, nb, n_tok, HEAD_DIM), (1, 0, 2, 3)),
                                     ((0, 0), (0, 0), (0, PAGE_SIZE - n_tok), (0, 0)))
        head_major_pool = lambda c: jnp.transpose(c, (0, 2, 1, 3))
        o_a_s = _dsa_sample(page_table, a_idx, w_idx, qblk,
                            jnp.swapaxes(cache_idx_k[l], 1, 2), ki_new,
                            head_major_pool(cache_k[l]), new_page(s["kab"]),
                            head_major_pool(cache_v[l]), new_page(s["vab"]), n_tok)
        o_r_s, s_new = _retention_sample(pad_tok(s["qr"]), pad_tok(s["kr"]), pad_tok(s["vg"][:, :R_WIDTH]),
                                         pad_tok(s["vg"][:, R_WIDTH:]), state_ret[l], n_tok)
        qblk_m = _block_diag_q(s["qm"].reshape(nb, n_tok, M_WIDTH), M_HEADS)
        o_m_s = _mem_attn_sample(qblk_m, cache_mem_k[l], cache_mem_v[l], n_tok)
        unpad = lambda a: a[:, :n_tok, :].reshape(ts, -1)
        y_s = _finish(y_s, s["h"], unpad(o_a_s), unpad(o_r_s), unpad(o_m_s), w, norm2[l], ts)
        outs[6].append(jnp.transpose(s["ka"], (1, 0, 2)).reshape(nb, n_tok, A_HEADS, HEAD_DIM))
        outs[7].append(jnp.transpose(s["va"], (1, 0, 2)).reshape(nb, n_tok, A_HEADS, HEAD_DIM))
        outs[8].append(s["kt"][:IDX_DIM].T.reshape(nb, n_tok, IDX_DIM))
        outs[9].append(s_new)
    stk = [jnp.stack(o) for o in outs]
    return (y_p.reshape(bp, seq, D_MODEL), y_s.reshape(nb, n_tok, D_MODEL), *stk)
```

```python
import functools

import jax
import jax.numpy as jnp
import numpy as np
from jax import lax
from jax.experimental import pallas as pl
from jax.experimental.pallas import tpu as pltpu

F32 = jnp.float32
BF16 = jnp.bfloat16
I32 = jnp.int32

D_MODEL = 2048
PAST_LEN = 2048
PAGE_SIZE = 128
N_PAGES = PAST_LEN // PAGE_SIZE
HEAD_DIM = 128
A_HEADS = 6
A_WIDTH = A_HEADS * HEAD_DIM
IDX_HEADS = 16
IDX_DIM = 64
TOPK_MAX = 256
R_HEADS = 6
R_DK = 128
R_WIDTH = R_HEADS * 128
RET_CHUNK = 128
ROPE_BASE = 10000.0
M_HEADS = 4
M_WIDTH = M_HEADS * HEAD_DIM
N_BRANCH = 3
N_GROUPS = 4
EXP_PER_GROUP = 8
N_EXPERTS = N_GROUPS * EXP_PER_GROUP
EXP_FF = 256
RMS_EPS = 1e-6

LANES = 128
INT_MIN = -(2 ** 31)
NEG_BIG = -1e30
LOG2_E = 1.4426950408889634
V7X_VMEM_LIMIT = 56 * 1024 * 1024

_SIZES = (A_WIDTH, A_WIDTH, A_WIDTH, IDX_HEADS * IDX_DIM, IDX_DIM, IDX_HEADS,
          R_HEADS * R_DK, R_HEADS * R_DK, R_WIDTH, R_WIDTH, M_WIDTH, N_BRANCH * D_MODEL)
_OFFS = tuple(int(v) for v in np.cumsum((0,) + _SIZES))
(O_QA, O_KA, O_VA, O_QI, O_KI, O_WI, O_QR, O_KR, O_VR, O_GR, O_QM, O_GATES, O_END) = _OFFS


def _params(n_grid, vmem=V7X_VMEM_LIMIT):
    return pltpu.CompilerParams(dimension_semantics=("arbitrary",) * n_grid, vmem_limit_bytes=vmem)


def _nt_dot(a, b):
    return lax.dot_general(a, b, (((1,), (1,)), ((), ())), preferred_element_type=F32)


def _dot(a, b):
    return jnp.dot(a, b, preferred_element_type=F32)


def _rmsnorm_kernel(x_ref, g_ref, o_ref):
    x = x_ref[...]
    ms = jnp.mean(x * x, axis=-1, keepdims=True)
    o_ref[...] = (x * lax.rsqrt(ms + RMS_EPS) * g_ref[...]).astype(o_ref.dtype)


def _rmsnorm_cast(x, g, tm):
    m, d = x.shape
    return pl.pallas_call(
        _rmsnorm_kernel,
        grid=(m // tm,),
        in_specs=[pl.BlockSpec((tm, d), lambda i: (i, 0)), pl.BlockSpec((1, d), lambda i: (0, 0))],
        out_specs=pl.BlockSpec((tm, d), lambda i: (i, 0)),
        out_shape=jax.ShapeDtypeStruct((m, d), BF16),
        compiler_params=_params(1),
        name="rmsnorm_cast",
    )(x, g.reshape(1, d))


def _store_cols(o_ref, hh, z):
    if len(o_ref.shape) == 3:
        o_ref[hh] = z.astype(o_ref.dtype)
    else:
        o_ref[:, hh * HEAD_DIM:(hh + 1) * HEAD_DIM] = z.astype(o_ref.dtype)


def _proj_plain_kernel(h_ref, w_ref, *o_refs):
    y = _dot(h_ref[...], w_ref[...])
    for o in o_refs:
        if len(o.shape) == 3:
            for hh in range(o.shape[0]):
                o[hh] = y[:, hh * HEAD_DIM:(hh + 1) * HEAD_DIM].astype(o.dtype)
        else:
            o[...] = y.astype(o.dtype)


def _proj_headnorm_kernel(h_ref, w_ref, g_ref, *o_refs, scale):
    y = _dot(h_ref[...], w_ref[...])
    g = g_ref[...]
    for hh in range(y.shape[1] // HEAD_DIM):
        sl = slice(hh * HEAD_DIM, (hh + 1) * HEAD_DIM)
        yh = y[:, sl]
        ms = jnp.mean(yh * yh, axis=-1, keepdims=True)
        z = yh * lax.rsqrt(ms + RMS_EPS) * g
        if scale != 1.0:
            z = z * scale
        for o in o_refs:
            _store_cols(o, hh, z)


def _proj_rope_kernel(h_ref, w_ref, cos_ref, sin_ref, o_ref, *, scale):
    y = _dot(h_ref[...], w_ref[...])
    c = cos_ref[...]
    s = sin_ref[...]
    for hh in range(y.shape[1] // R_DK):
        sl = slice(hh * R_DK, (hh + 1) * R_DK)
        yh = y[:, sl]
        z = yh * c + pltpu.roll(yh, R_DK // 2, 1) * s
        if scale != 1.0:
            z = z * scale
        o_ref[:, sl] = z


def _proj_kw_kernel(h_ref, w_ref, g_ref, kw_ref, kt_ref, kte_ref, kto_ref):
    y = _dot(h_ref[...], w_ref[...])
    lane = lax.broadcasted_iota(I32, y.shape, 1)
    is_k = lane < IDX_DIM
    ms = jnp.sum(jnp.where(is_k, y * y, 0.0), axis=-1, keepdims=True) * (1.0 / IDX_DIM)
    kn = y * lax.rsqrt(ms + RMS_EPS) * g_ref[...]
    kw_ref[...] = jnp.where(is_k, kn, y * (IDX_HEADS * IDX_DIM) ** -0.5)
    et = jnp.where(is_k, kn, 0.0).T
    kt_ref[...] = et
    kte_ref[...] = et.astype(BF16)
    kto_ref[...] = pltpu.roll(et, IDX_DIM, 0).astype(BF16)


ROW, TRANSPOSED, HEAD_MAJOR = "row", "transposed", "head_major"


def _proj(kern, h, w, extras, extra_specs, outs, tm, tn, name):
    m, k = h.shape
    n = w.shape[1]
    out_shape, out_specs = [], []
    for dt, layout in outs:
        if layout == TRANSPOSED:
            out_shape.append(jax.ShapeDtypeStruct((n, m), dt))
            out_specs.append(pl.BlockSpec((tn, tm), lambda j, i: (j, i)))
        elif layout == HEAD_MAJOR:
            out_shape.append(jax.ShapeDtypeStruct((n // HEAD_DIM, m, HEAD_DIM), dt))
            out_specs.append(pl.BlockSpec((tn // HEAD_DIM, tm, HEAD_DIM), lambda j, i: (j, i, 0)))
        else:
            out_shape.append(jax.ShapeDtypeStruct((m, n), dt))
            out_specs.append(pl.BlockSpec((tm, tn), lambda j, i: (i, j)))
    res = pl.pallas_call(
        kern,
        grid=(n // tn, m // tm),
        in_specs=[pl.BlockSpec((tm, k), lambda j, i: (i, 0)),
                  pl.BlockSpec((k, tn), lambda j, i: (0, j))] + list(extra_specs),
        out_specs=out_specs,
        out_shape=out_shape,
        compiler_params=_params(2),
        name=name,
    )(h, w, *extras)
    return res


def _row_spec(tm, width):
    return pl.BlockSpec((tm, width), lambda j, i: (i, 0))


def _const_spec(shape):
    return pl.BlockSpec(shape, lambda j, i: (0,) * len(shape))


def _score_key(score):
    bits = lax.bitcast_convert_type(score, I32)
    return jnp.where(bits < 0, bits ^ jnp.int32(0x7FFFFFFF), bits)


def _dsa_prompt_kernel(qa_ref, qi_ref, kw_ref, kte_ref, kto_ref, k_ref, v_ref, o_ref,
                       keys_ref, wb_ref, m_ref, l_ref, acc_ref, a_ref, s_ref, p_ref, *, topk, tq, ts, tk):
    i = pl.program_id(0)
    n_vis = i * tq + tq
    n_sc = (n_vis + ts - 1) // ts
    n_ch = (n_vis + tk - 1) // tk
    kw = kw_ref[...]
    for hh in range(IDX_HEADS):
        wb_ref[hh] = jnp.broadcast_to(kw[:, IDX_DIM + hh:IDX_DIM + hh + 1], (tq, LANES))
    row = i * tq + lax.broadcasted_iota(I32, (tq, LANES), 0)
    lane = lax.broadcasted_iota(I32, (tq, LANES), 1)

    def score_chunk(c, carry):
        off = pl.multiple_of(c * ts, ts)
        kte = kte_ref[:, pl.ds(off, ts)]
        kto = kto_ref[:, pl.ds(off, ts)]
        acc = [jnp.zeros((tq, LANES), F32) for _ in range(ts // LANES)]
        for p in range(IDX_HEADS // 2):
            lhs = qi_ref[:, p * LANES:(p + 1) * LANES]
            for w_head, rhs in ((wb_ref[2 * p], kte), (wb_ref[2 * p + 1], kto)):
                r = jnp.maximum(_dot(lhs, rhs), 0.0)
                for u in range(ts // LANES):
                    acc[u] = acc[u] + w_head * r[:, u * LANES:(u + 1) * LANES]
        for u in range(ts // LANES):
            col = off + u * LANES + lane
            keys_ref[:, pl.ds(off + u * LANES, LANES)] = jnp.where(col <= row, _score_key(acc[u]), INT_MIN)
        return carry

    lax.fori_loop(0, n_sc, score_chunk, 0)

    @pl.when(n_sc * ts < n_ch * tk)
    def _():
        keys_ref[:, pl.ds(pl.multiple_of(n_sc * ts, ts), ts)] = jnp.full((tq, ts), INT_MIN, I32)

    def bit_body(it, ua):
        cand_u = ua | jnp.left_shift(jnp.int32(1), 31 - it)
        cand = cand_u ^ INT_MIN

        def count_chunk(c, cnt):
            off = pl.multiple_of(c * tk, tk)
            for u in range(tk // LANES):
                cnt = cnt + jnp.where(keys_ref[:, pl.ds(off + u * LANES, LANES)] >= cand, 1.0, 0.0)
            return cnt

        cnt = lax.fori_loop(0, n_ch, count_chunk, jnp.zeros((tq, LANES), F32))
        total = jnp.sum(cnt, axis=1, keepdims=True)
        return jnp.where(total >= topk, cand_u, ua)

    ua = lax.fori_loop(0, 32, bit_body, jnp.zeros((tq, LANES), I32))
    thr = jnp.maximum(ua ^ INT_MIN, INT_MIN + 1)

    m_ref[...] = jnp.full(m_ref.shape, NEG_BIG, F32)
    l_ref[...] = jnp.zeros(l_ref.shape, F32)
    acc_ref[...] = jnp.zeros(acc_ref.shape, F32)

    def att_chunk(c, carry):
        off = pl.multiple_of(c * tk, tk)
        n_u = tk // LANES
        bias = jnp.concatenate(
            [jnp.where(keys_ref[:, pl.ds(off + u * LANES, LANES)] >= thr, 0.0, NEG_BIG) for u in range(n_u)], axis=1)
        for hh in range(A_HEADS):
            s_ref[hh] = _nt_dot(qa_ref[:, hh * HEAD_DIM:(hh + 1) * HEAD_DIM], k_ref[hh, pl.ds(off, tk), :]) + bias
        for hh in range(A_HEADS):
            m_old = m_ref[hh]
            mx = s_ref[hh, :, 0:LANES]
            for u in range(1, n_u):
                mx = jnp.maximum(mx, s_ref[hh, :, u * LANES:(u + 1) * LANES])
            m_new = jnp.maximum(m_old, jnp.broadcast_to(jnp.max(mx, axis=1, keepdims=True), (tq, LANES)))
            alpha = jnp.exp2(m_old - m_new)
            psum = jnp.zeros((tq, LANES), F32)
            for u in range(n_u):
                p = jnp.exp2(s_ref[hh, :, u * LANES:(u + 1) * LANES] - m_new)
                psum = psum + p
                p_ref[hh, :, u * LANES:(u + 1) * LANES] = p.astype(BF16)
            l_ref[hh] = alpha * l_ref[hh] + jnp.broadcast_to(jnp.sum(psum, axis=1, keepdims=True), (tq, LANES))
            m_ref[hh] = m_new
            a_ref[hh] = alpha
        for hh in range(A_HEADS):
            acc_ref[hh] = a_ref[hh] * acc_ref[hh] + _dot(p_ref[hh], v_ref[hh, pl.ds(off, tk), :])
        return carry

    lax.fori_loop(0, n_ch, att_chunk, 0)
    for hh in range(A_HEADS):
        o_ref[:, hh * HEAD_DIM:(hh + 1) * HEAD_DIM] = (acc_ref[hh] / l_ref[hh]).astype(o_ref.dtype)


def _dsa_prompt(qa, qi, kw, kte, kto, kb, vb):
    t = qa.shape[0]
    tq, ts, tk = 128, 256, 512
    assert t % tk == 0
    topk = min(TOPK_MAX, t // 4)
    one = pl.Buffered(1)
    head_state = pltpu.VMEM((A_HEADS, tq, LANES), F32)
    return pl.pallas_call(
        functools.partial(_dsa_prompt_kernel, topk=topk, tq=tq, ts=ts, tk=tk),
        grid=(t // tq,),
        in_specs=[pl.BlockSpec((tq, A_WIDTH), lambda i: (i, 0)),
                  pl.BlockSpec((tq, IDX_HEADS * IDX_DIM), lambda i: (i, 0)),
                  pl.BlockSpec((tq, LANES), lambda i: (i, 0)),
                  pl.BlockSpec((LANES, t), lambda i: (0, 0), pipeline_mode=one),
                  pl.BlockSpec((LANES, t), lambda i: (0, 0), pipeline_mode=one),
                  pl.BlockSpec((A_HEADS, t, HEAD_DIM), lambda i: (0, 0, 0), pipeline_mode=one),
                  pl.BlockSpec((A_HEADS, t, HEAD_DIM), lambda i: (0, 0, 0), pipeline_mode=one)],
        out_specs=pl.BlockSpec((tq, A_WIDTH), lambda i: (i, 0)),
        out_shape=jax.ShapeDtypeStruct((t, A_WIDTH), BF16),
        scratch_shapes=[pltpu.VMEM((tq, t), I32), pltpu.VMEM((IDX_HEADS, tq, LANES), F32),
                        head_state, head_state, head_state, head_state,
                        pltpu.VMEM((A_HEADS, tq, tk), F32), pltpu.VMEM((A_HEADS, tq, tk), BF16)],
        compiler_params=_params(1),
        name="dsa_prompt",
    )(qa, qi, kw, kte, kto, kb, vb)


def _dsa_sample_scores_kernel(pt_ref, a_ref, w_ref, *refs, n_tok):
    idx_pages, ki_new, keys_ref = refs[:N_PAGES], refs[N_PAGES], refs[N_PAGES + 1]
    a = a_ref[...]
    wcol = jnp.broadcast_to(w_ref[...], (n_tok * IDX_HEADS, LANES))
    sub8 = lax.broadcasted_iota(I32, (8, LANES), 0)
    lane8 = lax.broadcasted_iota(I32, (8, LANES), 1)
    for j in range(N_PAGES + 1):
        kp = idx_pages[j][...].astype(BF16) if j < N_PAGES else ki_new[...]
        r = jnp.maximum(_dot(a, kp), 0.0) * wcol
        k8 = jnp.full((8, LANES), INT_MIN, I32)
        for t in range(n_tok):
            st = jnp.sum(r[t * IDX_HEADS:(t + 1) * IDX_HEADS, :], axis=0, keepdims=True)
            kt = jnp.broadcast_to(_score_key(st), (8, LANES))
            if j == N_PAGES:
                kt = jnp.where(lane8 <= t, kt, INT_MIN)
            k8 = jnp.where(sub8 == t, kt, k8)
        keys_ref[:, j * LANES:(j + 1) * LANES] = k8


def _threshold_kernel(keys_ref, thr_ref, *, topk):
    rows, n_keys = keys_ref.shape

    def bit_body(it, ua):
        cand_u = ua | jnp.left_shift(jnp.int32(1), 31 - it)
        cand = cand_u ^ INT_MIN
        cnt = jnp.zeros((rows, LANES), F32)
        for j in range(n_keys // LANES):
            cnt = cnt + jnp.where(keys_ref[:, j * LANES:(j + 1) * LANES] >= cand, 1.0, 0.0)
        total = jnp.sum(cnt, axis=1, keepdims=True)
        return jnp.where(total >= topk, cand_u, ua)

    ua = lax.fori_loop(0, 32, bit_body, jnp.zeros((rows, LANES), I32))
    thr_ref[...] = jnp.maximum(ua ^ INT_MIN, INT_MIN + 1)


def _dsa_sample_attend_kernel(pt_ref, q_ref, keys_ref, thr_ref, *refs):
    n_pg = N_PAGES
    k_pages, k_new = refs[:n_pg], refs[n_pg]
    v_pages, v_new = refs[n_pg + 1:2 * n_pg + 1], refs[2 * n_pg + 1]
    o_ref, lg_ref, pn_ref = refs[2 * n_pg + 2:]
    thr = thr_ref[...]
    for j in range(n_pg + 1):
        page = k_pages[j] if j < n_pg else k_new
        bias8 = jnp.where(keys_ref[:, j * LANES:(j + 1) * LANES] >= thr, 0.0, NEG_BIG)
        bias = jnp.concatenate([bias8, bias8], axis=0)
        for hh in range(A_HEADS):
            s = _nt_dot(q_ref[:, hh * HEAD_DIM:(hh + 1) * HEAD_DIM], page[hh].astype(BF16))
            lg_ref[hh, :, j * LANES:(j + 1) * LANES] = s + bias
    for hh in range(A_HEADS):
        lg = lg_ref[hh]
        p = jnp.exp2(lg - jnp.max(lg, axis=1, keepdims=True))
        pn_ref[hh] = (p / jnp.sum(p, axis=1, keepdims=True)).astype(BF16)
    acc = [jnp.zeros((16, HEAD_DIM), F32) for _ in range(A_HEADS)]
    for j in range(n_pg + 1):
        page = v_pages[j] if j < n_pg else v_new
        for hh in range(A_HEADS):
            acc[hh] = acc[hh] + _dot(pn_ref[hh, :, j * LANES:(j + 1) * LANES], page[hh].astype(BF16))
    o_ref[...] = jnp.concatenate(acc, axis=1)


def _dsa_sample(page_table, a, wcol, q16, idx_pool, ki_new, k_pool, k_new, v_pool, v_new, n_tok):
    nb = a.shape[0]
    topk = min(TOPK_MAX, (PAST_LEN + n_tok) // 4)
    n_keys = (N_PAGES + 1) * LANES

    def batch_spec(rows, width):
        return pl.BlockSpec((None, rows, width), lambda b, pt: (b, 0, 0))

    idx_page = lambda j: pl.BlockSpec((None, IDX_DIM, PAGE_SIZE), lambda b, pt, j=j: (pt[b, j], 0, 0))
    keys = pl.pallas_call(
        functools.partial(_dsa_sample_scores_kernel, n_tok=n_tok),
        grid_spec=pltpu.PrefetchScalarGridSpec(
            num_scalar_prefetch=1, grid=(nb,),
            in_specs=[batch_spec(n_tok * IDX_HEADS, IDX_DIM), batch_spec(n_tok * IDX_HEADS, 1)]
            + [idx_page(j) for j in range(N_PAGES)] + [batch_spec(IDX_DIM, PAGE_SIZE)],
            out_specs=batch_spec(8, n_keys)),
        out_shape=jax.ShapeDtypeStruct((nb, 8, n_keys), I32),
        compiler_params=_params(1),
        name="dsa_sample_scores",
    )(page_table, a, wcol, *([idx_pool] * N_PAGES), ki_new)

    rows = nb * 8
    tr = min(rows, 256)
    thr = pl.pallas_call(
        functools.partial(_threshold_kernel, topk=topk),
        grid=(rows // tr,),
        in_specs=[pl.BlockSpec((tr, n_keys), lambda i: (i, 0))],
        out_specs=pl.BlockSpec((tr, LANES), lambda i: (i, 0)),
        out_shape=jax.ShapeDtypeStruct((rows, LANES), I32),
        compiler_params=_params(1),
        name="dsa_sample_threshold",
    )(keys.reshape(rows, n_keys)).reshape(nb, 8, LANES)

    kv_page = lambda j: pl.BlockSpec((None, A_HEADS, PAGE_SIZE, HEAD_DIM), lambda b, pt, j=j: (pt[b, j], 0, 0, 0))
    kv_new = pl.BlockSpec((None, A_HEADS, PAGE_SIZE, HEAD_DIM), lambda b, pt: (b, 0, 0, 0))
    return pl.pallas_call(
        _dsa_sample_attend_kernel,
        grid_spec=pltpu.PrefetchScalarGridSpec(
            num_scalar_prefetch=1, grid=(nb,),
            in_specs=[batch_spec(16, A_WIDTH), batch_spec(8, n_keys), batch_spec(8, LANES)]
            + [kv_page(j) for j in range(N_PAGES)] + [kv_new] + [kv_page(j) for j in range(N_PAGES)] + [kv_new],
            out_specs=batch_spec(16, A_WIDTH),
            scratch_shapes=[pltpu.VMEM((A_HEADS, 16, n_keys), F32), pltpu.VMEM((A_HEADS, 16, n_keys), BF16)]),
        out_shape=jax.ShapeDtypeStruct((nb, 16, A_WIDTH), F32),
        compiler_params=_params(1),
        name="dsa_sample_attend",
    )(page_table, q16, keys, thr, *([k_pool] * N_PAGES), k_new, *([v_pool] * N_PAGES), v_new)


def _ret_consts(chunk):
    lg = jnp.log1p(-jnp.exp2(-5.0 - jnp.arange(R_HEADS, dtype=F32)))
    n = jnp.arange(chunk, dtype=F32)
    diff = n[:, None] - n[None, :]
    dmat = jnp.where(diff >= 0, jnp.exp(lg[:, None, None] * jnp.maximum(diff, 0.0)[None]), 0.0)
    cdec = jnp.exp(lg[:, None] * (n[None, :] + 1.0))
    kdec = jnp.exp(lg[:, None] * (chunk - 1.0 - n[None, :]))
    sdec = jnp.exp(lg * chunk)
    return dmat, cdec, kdec, sdec


def _ret_out_store(o, g, o_ref, sl):
    on = o * lax.rsqrt(jnp.mean(o * o, axis=-1, keepdims=True) + RMS_EPS)
    o_ref[:, sl] = (on * (g * jax.nn.sigmoid(g))).astype(o_ref.dtype)


def _ret_prompt_kernel(q_ref, k_ref, v_ref, g_ref, dmat_ref, cdec_ref, kdec_ref, sdec_ref,
                       o_ref, sout_ref, s_ref):
    c = pl.program_id(0)

    @pl.when(c == 0)
    def _():
        s_ref[...] = jnp.zeros_like(s_ref)

    for hh in range(R_HEADS):
        sl = slice(hh * R_DK, (hh + 1) * R_DK)
        k = k_ref[:, sl]
        qb = q_ref[:, sl].astype(BF16)
        vb = v_ref[:, sl].astype(BF16)
        att = _nt_dot(qb, k.astype(BF16)) * dmat_ref[hh]
        intra = _dot(att.astype(BF16), vb)
        s0 = s_ref[hh]
        cross = _dot(qb, s0.astype(BF16)) * cdec_ref[hh]
        kd_t = (k * kdec_ref[hh]).T.astype(BF16)
        s_ref[hh] = sdec_ref[hh] * s0 + _dot(kd_t, vb)
        _ret_out_store(intra + cross, g_ref[:, sl], o_ref, sl)

    @pl.when(c == pl.num_programs(0) - 1)
    def _():
        sout_ref[...] = s_ref[...]


def _retention_prompt(qr, kr, vg):
    t = qr.shape[0]
    c = RET_CHUNK
    dmat, cdec, kdec, sdec = _ret_consts(c)
    rep = lambda a: jnp.broadcast_to(a[:, :, None], (R_HEADS, c, LANES))
    cdec_b, kdec_b = rep(cdec), rep(kdec)
    sdec_b = jnp.broadcast_to(sdec[:, None, None], (R_HEADS, R_DK, LANES))
    blk = lambda col: pl.BlockSpec((c, R_WIDTH), lambda i, col=col: (i, col))
    const = lambda: pl.BlockSpec((R_HEADS, c, LANES), lambda i: (0, 0, 0))
    return pl.pallas_call(
        _ret_prompt_kernel,
        grid=(t // c,),
        in_specs=[blk(0), blk(0), blk(0), blk(1), const(), const(), const(), const()],
        out_specs=[pl.BlockSpec((c, R_WIDTH), lambda i: (i, 0)),
                   pl.BlockSpec((R_HEADS, R_DK, LANES), lambda i: (0, 0, 0))],
        out_shape=[jax.ShapeDtypeStruct((t, R_WIDTH), BF16),
                   jax.ShapeDtypeStruct((R_HEADS, R_DK, 128), F32)],
        scratch_shapes=[pltpu.VMEM((R_HEADS, R_DK, 128), F32)],
        compiler_params=_params(1),
        name="retention_prompt",
    )(qr, kr, vg, vg, dmat, cdec_b, kdec_b, sdec_b)


def _ret_sample_kernel(q_ref, k_ref, v_ref, g_ref, s0_ref, dmat_ref, cdec_ref, kdec_ref, sdec_ref,
                       o_ref, sout_ref, kpad_ref, vpad_ref):
    @pl.when(pl.program_id(0) == 0)
    def _():
        kpad_ref[...] = jnp.zeros_like(kpad_ref)
        vpad_ref[...] = jnp.zeros_like(vpad_ref)

    kpad_ref[0:8, :] = k_ref[...]
    vpad_ref[0:8, :] = v_ref[...]
    for hh in range(R_HEADS):
        sl = slice(hh * R_DK, (hh + 1) * R_DK)
        kp = kpad_ref[:, sl]
        vb = vpad_ref[:, sl].astype(BF16)
        qb = q_ref[:, sl].astype(BF16)
        att = _nt_dot(qb, kp.astype(BF16)) * dmat_ref[hh]
        intra = _dot(att.astype(BF16), vb)
        s0 = s0_ref[hh]
        cross = _dot(qb, s0.astype(BF16)) * cdec_ref[hh]
        kd_t = (kp * kdec_ref[hh]).T.astype(BF16)
        sout_ref[hh] = sdec_ref[hh] * s0 + _dot(kd_t, vb)
        _ret_out_store(intra + cross, g_ref[:, sl], o_ref, sl)


def _retention_sample(qr, kr, vr, gr, state, n_tok):
    nb = qr.shape[0]
    dmat, cdec, kdec, sdec = _ret_consts(n_tok)
    dmat_p = jnp.zeros((R_HEADS, 8, LANES), F32).at[:, :n_tok, :n_tok].set(dmat)
    cdec_p = jnp.zeros((R_HEADS, 8, LANES), F32).at[:, :n_tok, :].set(
        jnp.broadcast_to(cdec[:, :, None], (R_HEADS, n_tok, LANES)))
    kdec_p = jnp.zeros((R_HEADS, LANES, LANES), F32).at[:, :n_tok, :].set(
        jnp.broadcast_to(kdec[:, :, None], (R_HEADS, n_tok, LANES)))
    sdec_b = jnp.broadcast_to(sdec[:, None, None], (R_HEADS, R_DK, LANES))
    tok = lambda: pl.BlockSpec((None, 8, R_WIDTH), lambda b: (b, 0, 0))
    st = lambda: pl.BlockSpec((None, R_HEADS, R_DK, 128), lambda b: (b, 0, 0, 0))
    const = lambda rows: pl.BlockSpec((R_HEADS, rows, LANES), lambda b: (0, 0, 0))
    return pl.pallas_call(
        _ret_sample_kernel,
        grid=(nb,),
        in_specs=[tok(), tok(), tok(), tok(), st(), const(8), const(8), const(LANES), const(R_DK)],
        out_specs=[tok(), st()],
        out_shape=[jax.ShapeDtypeStruct((nb, 8, R_WIDTH), F32),
                   jax.ShapeDtypeStruct(state.shape, F32)],
        scratch_shapes=[pltpu.VMEM((LANES, R_WIDTH), F32), pltpu.VMEM((LANES, R_WIDTH), F32)],
        compiler_params=_params(1),
        name="retention_sample",
    )(qr, kr, vr, gr, state, dmat_p, cdec_p, kdec_p, sdec_b)


def _mem_attn_prompt_kernel(q_ref, k_ref, v_ref, o_ref):
    for hh in range(M_HEADS):
        sl = slice(hh * HEAD_DIM, (hh + 1) * HEAD_DIM)
        s = _nt_dot(q_ref[:, sl], k_ref[:, sl])
        p = jnp.exp2(s - jnp.max(s, axis=1, keepdims=True))
        pn = (p / jnp.sum(p, axis=1, keepdims=True)).astype(BF16)
        o_ref[:, sl] = _dot(pn, v_ref[:, sl]).astype(o_ref.dtype)


def _mem_attn_prompt(qm, mk, mv, tm):
    t = qm.shape[0]
    nm = mk.shape[0]
    return pl.pallas_call(
        _mem_attn_prompt_kernel,
        grid=(t // tm,),
        in_specs=[pl.BlockSpec((tm, M_WIDTH), lambda i: (i, 0)),
                  pl.BlockSpec((nm, M_WIDTH), lambda i: (0, 0)),
                  pl.BlockSpec((nm, M_WIDTH), lambda i: (0, 0))],
        out_specs=pl.BlockSpec((tm, M_WIDTH), lambda i: (i, 0)),
        out_shape=jax.ShapeDtypeStruct((t, M_WIDTH), BF16),
        compiler_params=_params(1),
        name="mem_attn_prompt",
    )(qm, mk, mv)


def _mem_attn_sample_kernel(q_ref, k_ref, v_ref, o_ref):
    nm = k_ref.shape[1] // M_HEADS
    for g in range(q_ref.shape[0]):
        outs = []
        for hh in range(M_HEADS):
            rows = pl.ds(hh, nm, stride=M_HEADS)
            s = _nt_dot(q_ref[g, :, hh * HEAD_DIM:(hh + 1) * HEAD_DIM], k_ref[g, rows, :].astype(BF16))
            p = jnp.exp2(s - jnp.max(s, axis=1, keepdims=True))
            pn = (p / jnp.sum(p, axis=1, keepdims=True)).astype(BF16)
            outs.append(_dot(pn, v_ref[g, rows, :].astype(BF16)))
        o_ref[g] = jnp.concatenate(outs, axis=1)


def _mem_attn_sample(q16, mem_k, mem_v):
    nb, nm = mem_k.shape[0], mem_k.shape[1]
    gb = 4 if nb % 4 == 0 else 1
    return pl.pallas_call(
        _mem_attn_sample_kernel,
        grid=(nb // gb,),
        in_specs=[pl.BlockSpec((gb, 16, M_WIDTH), lambda b: (b, 0, 0)),
                  pl.BlockSpec((gb, nm, HEAD_DIM), lambda b: (b, 0, 0)),
                  pl.BlockSpec((gb, nm, HEAD_DIM), lambda b: (b, 0, 0))],
        out_specs=pl.BlockSpec((gb, 16, M_WIDTH), lambda b: (b, 0, 0)),
        out_shape=jax.ShapeDtypeStruct((nb, 16, M_WIDTH), F32),
        compiler_params=_params(1),
        name="mem_attn_sample",
    )(q16, mem_k, mem_v)


def _merged_kernel(h_ref, oa_ref, ob_ref, om_ref, wg0_ref, wg1_ref, wg2_ref, wa_ref, wb_ref, wm_ref, o_ref):
    h = h_ref[...]
    acc = None
    for wg, o, wbr in ((wg0_ref, oa_ref, wa_ref), (wg1_ref, ob_ref, wb_ref), (wg2_ref, om_ref, wm_ref)):
        term = jax.nn.sigmoid(_dot(h, wg[...])) * _dot(o[...].astype(BF16), wbr[...])
        acc = term if acc is None else acc + term
    o_ref[...] = acc.astype(o_ref.dtype)


def _merged(h, o_a, o_b, o_m, w_gates, w_a, w_b, w_m, tm, tn):
    m = h.shape[0]
    nj = D_MODEL // tn
    row = lambda width: pl.BlockSpec((tm, width), lambda j, i: (i, 0))
    gate = lambda b: pl.BlockSpec((D_MODEL, tn), lambda j, i, b=b: (0, b * nj + j))
    br = lambda width: pl.BlockSpec((width, tn), lambda j, i: (0, j))
    return pl.pallas_call(
        _merged_kernel,
        grid=(nj, m // tm),
        in_specs=[row(D_MODEL), row(A_WIDTH), row(R_WIDTH), row(M_WIDTH), gate(0), gate(1), gate(2),
                  br(A_WIDTH), br(R_WIDTH), br(M_WIDTH)],
        out_specs=pl.BlockSpec((tm, tn), lambda j, i: (i, j)),
        out_shape=jax.ShapeDtypeStruct((m, D_MODEL), BF16),
        compiler_params=_params(2),
        name="merged",
    )(h, o_a, o_b, o_m, w_gates, w_gates, w_gates, w_a, w_b, w_m)


def _outproj_kernel(x_ref, mg_ref, wo_ref, n2_ref, wrt_ref, x2_ref, h2_ref, rl_ref):
    x2 = x_ref[...] + _dot(mg_ref[...], wo_ref[...])
    x2_ref[...] = x2
    h2 = x2 * lax.rsqrt(jnp.mean(x2 * x2, axis=-1, keepdims=True) + RMS_EPS) * n2_ref[...]
    h2_ref[...] = h2.astype(BF16)
    rl_ref[...] = jnp.dot(h2, wrt_ref[...], preferred_element_type=F32, precision=lax.Precision.HIGHEST)


def _outproj(x, merged, w_out, norm2, w_rt, tm):
    m = x.shape[0]
    one = pl.Buffered(1)
    row = lambda width: pl.BlockSpec((tm, width), lambda i: (i, 0))
    return pl.pallas_call(
        _outproj_kernel,
        grid=(m // tm,),
        in_specs=[row(D_MODEL), row(D_MODEL),
                  pl.BlockSpec((D_MODEL, D_MODEL), lambda i: (0, 0), pipeline_mode=one),
                  pl.BlockSpec((1, D_MODEL), lambda i: (0, 0)),
                  pl.BlockSpec((D_MODEL, LANES), lambda i: (0, 0), pipeline_mode=one)],
        out_specs=[row(D_MODEL), row(D_MODEL), row(LANES)],
        out_shape=[jax.ShapeDtypeStruct((m, D_MODEL), F32), jax.ShapeDtypeStruct((m, D_MODEL), BF16),
                   jax.ShapeDtypeStruct((m, LANES), F32)],
        compiler_params=_params(1),
        name="outproj",
    )(x, merged, w_out, norm2.reshape(1, D_MODEL), w_rt)


def _router_kernel(rl_ref, comb_ref):
    z = rl_ref[...]
    lane = lax.broadcasted_iota(I32, z.shape, 1).astype(F32)
    ninf = -jnp.inf
    big = 1e9
    is_g = lane < N_GROUPS
    gl = jnp.where(is_g, z, ninf)
    gmax = jnp.max(gl, axis=1, keepdims=True)
    gsum = jnp.sum(jnp.where(is_g, jnp.exp(gl - gmax), 0.0), axis=1, keepdims=True)
    p_sel = 1.0 / gsum
    g_sel = jnp.min(jnp.where(gl == gmax, lane, big), axis=1, keepdims=True)
    lo = N_GROUPS + EXP_PER_GROUP * g_sel
    el = jnp.where(lane >= lo, jnp.where(lane < lo + EXP_PER_GROUP, z, ninf), ninf)
    v1 = jnp.max(el, axis=1, keepdims=True)
    i1 = jnp.min(jnp.where(el == v1, lane, big), axis=1, keepdims=True)
    el2 = jnp.where(lane == i1, ninf, el)
    v2 = jnp.max(el2, axis=1, keepdims=True)
    i2 = jnp.min(jnp.where(el2 == v2, lane, big), axis=1, keepdims=True)
    e2 = jnp.exp(v2 - v1)
    w1 = 1.0 / (1.0 + e2)
    w2 = e2 / (1.0 + e2)
    comb_ref[...] = jnp.where(lane == i1, w1 * p_sel, 0.0) + jnp.where(lane == i2, w2 * p_sel, 0.0)


def _router(rl, tm):
    m = rl.shape[0]
    return pl.pallas_call(
        _router_kernel,
        grid=(m // tm,),
        in_specs=[pl.BlockSpec((tm, LANES), lambda i: (i, 0))],
        out_specs=pl.BlockSpec((tm, LANES), lambda i: (i, 0)),
        out_shape=jax.ShapeDtypeStruct((m, LANES), F32),
        compiler_params=_params(1),
        name="router",
    )(rl)


def _moe_kernel(x2_ref, h2_ref, comb_ref, wg_ref, wu_ref, wd_ref, y_ref, acc_ref, *, ne):
    k = pl.program_id(1)

    @pl.when(k == 0)
    def _():
        acc_ref[...] = x2_ref[...]

    h2 = h2_ref[...]
    comb = comb_ref[...]
    lane = lax.broadcasted_iota(I32, comb.shape, 1)
    for e in range(ne):
        ce = jnp.sum(jnp.where(lane == N_GROUPS + k * ne + e, comb, 0.0), axis=1, keepdims=True)
        a = _dot(h2, wg_ref[e])
        u = _dot(h2, wu_ref[e])
        act = (a * jax.nn.sigmoid(a)) * u * ce
        acc_ref[...] += _dot(act.astype(BF16), wd_ref[e])

    @pl.when(k == pl.num_programs(1) - 1)
    def _():
        y_ref[...] = acc_ref[...]


def _moe(x2, h2, comb, wg, wu, wd, tm, ne):
    m = x2.shape[0]
    row = lambda width: pl.BlockSpec((tm, width), lambda i, k: (i, 0))
    return pl.pallas_call(
        functools.partial(_moe_kernel, ne=ne),
        grid=(m // tm, N_EXPERTS // ne),
        in_specs=[row(D_MODEL), row(D_MODEL), row(LANES),
                  pl.BlockSpec((ne, D_MODEL, EXP_FF), lambda i, k: (k, 0, 0)),
                  pl.BlockSpec((ne, D_MODEL, EXP_FF), lambda i, k: (k, 0, 0)),
                  pl.BlockSpec((ne, EXP_FF, D_MODEL), lambda i, k: (k, 0, 0))],
        out_specs=row(D_MODEL),
        out_shape=jax.ShapeDtypeStruct((m, D_MODEL), F32),
        scratch_shapes=[pltpu.VMEM((tm, D_MODEL), F32)],
        compiler_params=_params(2),
        name="moe",
    )(x2, h2, comb, wg, wu, wd)


def _rope_tables(pos):
    half = R_DK // 2
    freqs = ROPE_BASE ** (-jnp.arange(half, dtype=F32) / half)
    ang = pos.astype(F32)[:, None] * freqs[None, :]
    cos, sin = jnp.cos(ang), jnp.sin(ang)
    return jnp.concatenate([cos, cos], axis=1), jnp.concatenate([-sin, sin], axis=1)


def _layer_weights(l, w_in, w_mem_kv, w_br_a, w_br_b, w_br_m, w_out, w_group, w_router,
                   w_gate_e, w_up_e, w_down_e):
    wi = w_in[l]
    col = lambda a, b: wi[:, a:b].astype(BF16)
    w_kw = jnp.pad(wi[:, O_KI:O_QR], ((0, 0), (0, LANES - (O_QR - O_KI)))).astype(BF16)
    w_rt = jnp.pad(jnp.concatenate([w_group[l], w_router[l]], axis=1),
                   ((0, 0), (0, LANES - N_GROUPS - N_EXPERTS)))
    return dict(
        qa=col(O_QA, O_KA), ka=col(O_KA, O_VA), va=col(O_VA, O_QI), qi=col(O_QI, O_KI), kw=w_kw,
        qr=col(O_QR, O_KR), kr=col(O_KR, O_VR), vg=col(O_VR, O_QM), qm=col(O_QM, O_GATES),
        gates=col(O_GATES, O_END),
        mk=w_mem_kv[l][:, :M_WIDTH].astype(BF16), mv=w_mem_kv[l][:, M_WIDTH:].astype(BF16),
        br_a=w_br_a[l].astype(BF16), br_b=w_br_b[l].astype(BF16), br_m=w_br_m[l].astype(BF16),
        out=w_out[l].astype(BF16), rt=w_rt,
        ge=w_gate_e[l].astype(BF16), ue=w_up_e[l].astype(BF16), de=w_down_e[l].astype(BF16))


def _project(x2d, pos, w, norms, tm):
    t = x2d.shape[0]
    norm1, q_norm_a, k_norm_a, idx_k_norm, q_norm_m = norms
    h = _rmsnorm_cast(x2d, norm1, tm)
    hd = lambda g: g.reshape(1, HEAD_DIM)
    gspec = _const_spec((1, HEAD_DIM))
    scale = HEAD_DIM ** -0.5 * LOG2_E
    (qa,) = _proj(functools.partial(_proj_headnorm_kernel, scale=scale), h, w["qa"], [hd(q_norm_a)], [gspec],
                  [(BF16, ROW)], tm, A_WIDTH, "proj_qa")
    ka, kab = _proj(functools.partial(_proj_headnorm_kernel, scale=1.0), h, w["ka"], [hd(k_norm_a)], [gspec],
                    [(F32, HEAD_MAJOR), (BF16, HEAD_MAJOR)], tm, A_WIDTH, "proj_ka")
    va, vab = _proj(_proj_plain_kernel, h, w["va"], [], [], [(F32, HEAD_MAJOR), (BF16, HEAD_MAJOR)], tm, A_WIDTH,
                    "proj_va")
    (qi,) = _proj(_proj_plain_kernel, h, w["qi"], [], [], [(BF16, ROW)], tm, 512, "proj_qi")
    g_idx = jnp.pad(idx_k_norm, (0, LANES - IDX_DIM)).reshape(1, LANES)
    kw, kt, kte, kto = _proj(_proj_kw_kernel, h, w["kw"], [g_idx], [_const_spec((1, LANES))],
                             [(F32, ROW), (F32, TRANSPOSED), (BF16, TRANSPOSED), (BF16, TRANSPOSED)], tm, LANES,
                             "proj_kw")
    cos2, sin2 = _rope_tables(pos)
    tabs, tab_specs = [cos2, sin2], [_row_spec(tm, R_DK), _row_spec(tm, R_DK)]
    (qr,) = _proj(functools.partial(_proj_rope_kernel, scale=1.0), h, w["qr"], tabs, tab_specs,
                  [(F32, ROW)], tm, R_WIDTH, "proj_qr")
    (kr,) = _proj(functools.partial(_proj_rope_kernel, scale=R_DK ** -0.5), h, w["kr"], tabs, tab_specs,
                  [(F32, ROW)], tm, R_WIDTH, "proj_kr")
    (vg,) = _proj(_proj_plain_kernel, h, w["vg"], [], [], [(F32, ROW)], tm, R_WIDTH, "proj_vg")
    (qm,) = _proj(functools.partial(_proj_headnorm_kernel, scale=scale), h, w["qm"], [hd(q_norm_m)], [gspec],
                  [(BF16, ROW)], tm, M_WIDTH, "proj_qm")
    return dict(h=h, qa=qa, ka=ka, kab=kab, va=va, vab=vab, qi=qi, kw=kw, kt=kt, kte=kte, kto=kto,
                qr=qr, kr=kr, vg=vg, qm=qm)


def _finish(x2d, h, o_a, o_b, o_m, w, norm2, tm):
    merged = _merged(h, o_a, o_b, o_m, w["gates"], w["br_a"], w["br_b"], w["br_m"], tm, 512)
    x2, h2, rl = _outproj(x2d, merged, w["out"], norm2, w["rt"], min(tm, 256))
    comb = _router(rl, tm)
    return _moe(x2, h2, comb, w["ge"], w["ue"], w["de"], tm, 2)


def kernel(x_prompt, x_sample, mem_prompt, cache_k, cache_v, cache_idx_k, state_ret, cache_mem_k, cache_mem_v,
           page_table, norm1, w_in, q_norm_a, k_norm_a, idx_k_norm, q_norm_m, k_norm_m, mem_norm, w_mem_kv,
           w_br_a, w_br_b, w_br_m, w_out, norm2, w_group, w_router, w_gate_e, w_up_e, w_down_e):
    depth = w_in.shape[0]
    bp, seq, _ = x_prompt.shape
    nb, n_tok, _ = x_sample.shape
    assert bp == 1 and n_tok <= 8
    y_p = x_prompt.reshape(seq, D_MODEL)
    y_s = x_sample.reshape(nb * n_tok, D_MODEL)
    pos_p = jnp.arange(seq)
    pos_s = jnp.tile(PAST_LEN + jnp.arange(n_tok), nb)
    outs = [[] for _ in range(10)]
    for l in range(depth):
        w = _layer_weights(l, w_in, w_mem_kv, w_br_a, w_br_b, w_br_m, w_out, w_group, w_router,
                           w_gate_e, w_up_e, w_down_e)
        norms = (norm1[l], q_norm_a[l], k_norm_a[l], idx_k_norm[l], q_norm_m[l])

        tm = 512
        p = _project(y_p, pos_p, w, norms, tm)
        o_a = _dsa_prompt(p["qa"], p["qi"], p["kw"], p["kte"], p["kto"], p["kab"], p["vab"])
        o_r, s_fin = _retention_prompt(p["qr"], p["kr"], p["vg"])
        n_mem = mem_prompt.shape[1]
        hm = _rmsnorm_cast(mem_prompt.reshape(n_mem, D_MODEL), mem_norm[l], n_mem)
        mk, mkb = _proj(functools.partial(_proj_headnorm_kernel, scale=1.0), hm, w["mk"],
                        [k_norm_m[l].reshape(1, HEAD_DIM)], [_const_spec((1, HEAD_DIM))],
                        [(F32, ROW), (BF16, ROW)], n_mem, M_WIDTH, "proj_mk")
        mv, mvb = _proj(_proj_plain_kernel, hm, w["mv"], [], [], [(F32, ROW), (BF16, ROW)],
                        n_mem, M_WIDTH, "proj_mv")
        o_m = _mem_attn_prompt(p["qm"], mkb, mvb, tm)
        y_p = _finish(y_p, p["h"], o_a, o_r, o_m, w, norm2[l], tm)
        outs[0].append(jnp.transpose(p["ka"], (1, 0, 2)).reshape(bp, seq, A_HEADS, HEAD_DIM))
        outs[1].append(jnp.transpose(p["va"], (1, 0, 2)).reshape(bp, seq, A_HEADS, HEAD_DIM))
        outs[2].append(p["kt"][:IDX_DIM].T.reshape(bp, seq, IDX_DIM))
        outs[3].append(s_fin.reshape(bp, R_HEADS, R_DK, 128))
        outs[4].append(mk.reshape(bp, n_mem, M_HEADS, HEAD_DIM))
        outs[5].append(mv.reshape(bp, n_mem, M_HEADS, HEAD_DIM))

        ts = nb * n_tok
        s = _project(y_s, pos_s, w, norms, ts)
        pad_tok = lambda a: jnp.pad(a.reshape(nb, n_tok, -1), ((0, 0), (0, 8 - n_tok), (0, 0)))
        a_idx = s["qi"].reshape(nb, n_tok * IDX_HEADS, IDX_DIM)
        w_idx = s["kw"][:, IDX_DIM:IDX_DIM + IDX_HEADS].reshape(nb, n_tok * IDX_HEADS, 1)
        pad16 = lambda a: jnp.pad(a.reshape(nb, n_tok, -1), ((0, 0), (0, 16 - n_tok), (0, 0)))
        ki_new = jnp.pad(jnp.transpose(s["kte"][:IDX_DIM].reshape(IDX_DIM, nb, n_tok), (1, 0, 2)),
                         ((0, 0), (0, 0), (0, PAGE_SIZE - n_tok)))
        new_page = lambda a: jnp.pad(jnp.transpose(a.reshape(A_HEADS, nb, n_tok, HEAD_DIM), (1, 0, 2, 3)),
                                     ((0, 0), (0, 0), (0, PAGE_SIZE - n_tok), (0, 0)))
        head_major_pool = lambda c: jnp.transpose(c, (0, 2, 1, 3))
        o_a_s = _dsa_sample(page_table, a_idx, w_idx, pad16(s["qa"]),
                            jnp.swapaxes(cache_idx_k[l], 1, 2), ki_new,
                            head_major_pool(cache_k[l]), new_page(s["kab"]),
                            head_major_pool(cache_v[l]), new_page(s["vab"]), n_tok)
        o_r_s, s_new = _retention_sample(pad_tok(s["qr"]), pad_tok(s["kr"]), pad_tok(s["vg"][:, :R_WIDTH]),
                                         pad_tok(s["vg"][:, R_WIDTH:]), state_ret[l], n_tok)
        mem_rows = lambda c: c.reshape(nb, -1, HEAD_DIM)
        o_m_s = _mem_attn_sample(pad16(s["qm"]), mem_rows(cache_mem_k[l]), mem_rows(cache_mem_v[l]))
        unpad = lambda a: a[:, :n_tok, :].reshape(ts, -1)
        y_s = _finish(y_s, s["h"], unpad(o_a_s), unpad(o_r_s), unpad(o_m_s), w, norm2[l], ts)
        outs[6].append(jnp.transpose(s["ka"], (1, 0, 2)).reshape(nb, n_tok, A_HEADS, HEAD_DIM))
        outs[7].append(jnp.transpose(s["va"], (1, 0, 2)).reshape(nb, n_tok, A_HEADS, HEAD_DIM))
        outs[8].append(s["kt"][:IDX_DIM].T.reshape(nb, n_tok, IDX_DIM))
        outs[9].append(s_new)
    stk = [jnp.stack(o) for o in outs]
    return (y_p.reshape(bp, seq, D_MODEL), y_s.reshape(nb, n_tok, D_MODEL), *stk)
```

```python
import functools

import jax
import jax.numpy as jnp
import numpy as np
from jax import lax
from jax.experimental import pallas as pl
from jax.experimental.pallas import tpu as pltpu

F32 = jnp.float32
BF16 = jnp.bfloat16
I32 = jnp.int32

D_MODEL = 2048
PAST_LEN = 2048
PAGE_SIZE = 128
N_PAGES = PAST_LEN // PAGE_SIZE
HEAD_DIM = 128
A_HEADS = 6
A_WIDTH = A_HEADS * HEAD_DIM
IDX_HEADS = 16
IDX_DIM = 64
TOPK_MAX = 256
R_HEADS = 6
R_DK = 128
R_WIDTH = R_HEADS * 128
RET_CHUNK = 128
ROPE_BASE = 10000.0
M_HEADS = 4
M_WIDTH = M_HEADS * HEAD_DIM
N_BRANCH = 3
N_GROUPS = 4
EXP_PER_GROUP = 8
N_EXPERTS = N_GROUPS * EXP_PER_GROUP
EXP_FF = 256
RMS_EPS = 1e-6

LANES = 128
INT_MIN = -(2 ** 31)
NEG_BIG = -1e30
LOG2_E = 1.4426950408889634
V7X_VMEM_LIMIT = 56 * 1024 * 1024

_SIZES = (A_WIDTH, A_WIDTH, A_WIDTH, IDX_HEADS * IDX_DIM, IDX_DIM, IDX_HEADS,
          R_HEADS * R_DK, R_HEADS * R_DK, R_WIDTH, R_WIDTH, M_WIDTH, N_BRANCH * D_MODEL)
_OFFS = tuple(int(v) for v in np.cumsum((0,) + _SIZES))
(O_QA, O_KA, O_VA, O_QI, O_KI, O_WI, O_QR, O_KR, O_VR, O_GR, O_QM, O_GATES, O_END) = _OFFS


def _params(n_grid, vmem=V7X_VMEM_LIMIT):
    return pltpu.CompilerParams(dimension_semantics=("arbitrary",) * n_grid, vmem_limit_bytes=vmem)


def _nt_dot(a, b):
    return lax.dot_general(a, b, (((1,), (1,)), ((), ())), preferred_element_type=F32)


def _dot(a, b):
    return jnp.dot(a, b, preferred_element_type=F32)


def _rmsnorm_kernel(x_ref, g_ref, o_ref):
    x = x_ref[...]
    ms = jnp.mean(x * x, axis=-1, keepdims=True)
    o_ref[...] = (x * lax.rsqrt(ms + RMS_EPS) * g_ref[...]).astype(o_ref.dtype)


def _rmsnorm_cast(x, g, tm):
    m, d = x.shape
    return pl.pallas_call(
        _rmsnorm_kernel,
        grid=(m // tm,),
        in_specs=[pl.BlockSpec((tm, d), lambda i: (i, 0)), pl.BlockSpec((1, d), lambda i: (0, 0))],
        out_specs=pl.BlockSpec((tm, d), lambda i: (i, 0)),
        out_shape=jax.ShapeDtypeStruct((m, d), BF16),
        compiler_params=_params(1),
        name="rmsnorm_cast",
    )(x, g.reshape(1, d))


def _store_cols(o_ref, hh, z):
    if len(o_ref.shape) == 3:
        o_ref[hh] = z.astype(o_ref.dtype)
    else:
        o_ref[:, hh * HEAD_DIM:(hh + 1) * HEAD_DIM] = z.astype(o_ref.dtype)


def _proj_plain_kernel(h_ref, w_ref, *o_refs):
    y = _dot(h_ref[...], w_ref[...])
    for o in o_refs:
        if len(o.shape) == 3:
            for hh in range(o.shape[0]):
                o[hh] = y[:, hh * HEAD_DIM:(hh + 1) * HEAD_DIM].astype(o.dtype)
        else:
            o[...] = y.astype(o.dtype)


def _proj_headnorm_kernel(h_ref, w_ref, g_ref, *o_refs, scale):
    y = _dot(h_ref[...], w_ref[...])
    g = g_ref[...]
    for hh in range(y.shape[1] // HEAD_DIM):
        sl = slice(hh * HEAD_DIM, (hh + 1) * HEAD_DIM)
        yh = y[:, sl]
        ms = jnp.mean(yh * yh, axis=-1, keepdims=True)
        z = yh * lax.rsqrt(ms + RMS_EPS) * g
        if scale != 1.0:
            z = z * scale
        for o in o_refs:
            _store_cols(o, hh, z)


def _proj_rope_kernel(h_ref, w_ref, cos_ref, sin_ref, o_ref, *, scale):
    y = _dot(h_ref[...], w_ref[...])
    c = cos_ref[...]
    s = sin_ref[...]
    for hh in range(y.shape[1] // R_DK):
        sl = slice(hh * R_DK, (hh + 1) * R_DK)
        yh = y[:, sl]
        z = yh * c + pltpu.roll(yh, R_DK // 2, 1) * s
        if scale != 1.0:
            z = z * scale
        o_ref[:, sl] = z


def _proj_kw_kernel(h_ref, w_ref, g_ref, kw_ref, kt_ref, kte_ref, kto_ref):
    y = _dot(h_ref[...], w_ref[...])
    lane = lax.broadcasted_iota(I32, y.shape, 1)
    is_k = lane < IDX_DIM
    ms = jnp.sum(jnp.where(is_k, y * y, 0.0), axis=-1, keepdims=True) * (1.0 / IDX_DIM)
    kn = y * lax.rsqrt(ms + RMS_EPS) * g_ref[...]
    kw_ref[...] = jnp.where(is_k, kn, y * (IDX_HEADS * IDX_DIM) ** -0.5)
    et = jnp.where(is_k, kn, 0.0).T
    kt_ref[...] = et
    kte_ref[...] = et.astype(BF16)
    kto_ref[...] = pltpu.roll(et, IDX_DIM, 0).astype(BF16)


ROW, TRANSPOSED, HEAD_MAJOR = "row", "transposed", "head_major"


def _proj(kern, h, w, extras, extra_specs, outs, tm, tn, name):
    m, k = h.shape
    n = w.shape[1]
    out_shape, out_specs = [], []
    for dt, layout in outs:
        if layout == TRANSPOSED:
            out_shape.append(jax.ShapeDtypeStruct((n, m), dt))
            out_specs.append(pl.BlockSpec((tn, tm), lambda j, i: (j, i)))
        elif layout == HEAD_MAJOR:
            out_shape.append(jax.ShapeDtypeStruct((n // HEAD_DIM, m, HEAD_DIM), dt))
            out_specs.append(pl.BlockSpec((tn // HEAD_DIM, tm, HEAD_DIM), lambda j, i: (j, i, 0)))
        else:
            out_shape.append(jax.ShapeDtypeStruct((m, n), dt))
            out_specs.append(pl.BlockSpec((tm, tn), lambda j, i: (i, j)))
    res = pl.pallas_call(
        kern,
        grid=(n // tn, m // tm),
        in_specs=[pl.BlockSpec((tm, k), lambda j, i: (i, 0)),
                  pl.BlockSpec((k, tn), lambda j, i: (0, j))] + list(extra_specs),
        out_specs=out_specs,
        out_shape=out_shape,
        compiler_params=_params(2),
        name=name,
    )(h, w, *extras)
    return res


def _row_spec(tm, width):
    return pl.BlockSpec((tm, width), lambda j, i: (i, 0))


def _const_spec(shape):
    return pl.BlockSpec(shape, lambda j, i: (0,) * len(shape))


def _score_key(score):
    bits = lax.bitcast_convert_type(score, I32)
    return jnp.where(bits < 0, bits ^ jnp.int32(0x7FFFFFFF), bits)


def _bisect_threshold(count_ge, rows, topk):
    def bit_body(it, carry):
        ua, cnt = carry
        cand_u = ua | jnp.left_shift(jnp.int32(1), 31 - it)
        total = count_ge(cand_u ^ INT_MIN)
        ok = total >= topk
        return jnp.where(ok, cand_u, ua), jnp.where(ok, jnp.broadcast_to(total, cnt.shape), cnt)

    init = (jnp.zeros((rows, LANES), I32), jnp.zeros((rows, LANES), F32))
    ua, cnt = lax.fori_loop(0, 32, bit_body, init)
    return jnp.maximum(ua ^ INT_MIN, INT_MIN + 1), cnt


def _demote_excess_ties(keys_ref, thr, need, n_pieces):
    rows = keys_ref.shape[0]
    tri = jnp.where(lax.broadcasted_iota(I32, (LANES, LANES), 0) <= lax.broadcasted_iota(I32, (LANES, LANES), 1),
                    1.0, 0.0).astype(BF16)

    def piece(j, seen):
        off = pl.multiple_of(j * LANES, LANES)
        k = keys_ref[:, pl.ds(off, LANES)]
        tie = jnp.where(k == thr, 1.0, 0.0)
        rank = seen + _dot(tie.astype(BF16), tri)
        keys_ref[:, pl.ds(off, LANES)] = jnp.where(tie * rank > need, INT_MIN, k)
        return jnp.broadcast_to(rank[:, LANES - 1:LANES], (rows, LANES))

    lax.fori_loop(0, n_pieces, piece, jnp.zeros((rows, LANES), F32))


def _dsa_prompt_kernel(qa_ref, qi_ref, kw_ref, kte_ref, kto_ref, k_ref, v_ref, o_ref,
                       keys_ref, wb_ref, m_ref, l_ref, acc_ref, a_ref, s_ref, p_ref, *, topk, tq, ts, tc, tk):
    i = pl.program_id(0)
    n_vis = i * tq + tq
    n_sc = (n_vis + ts - 1) // ts
    n_ch = (n_vis + tk - 1) // tk
    kw = kw_ref[...]
    for hh in range(IDX_HEADS):
        wb_ref[hh] = jnp.broadcast_to(kw[:, IDX_DIM + hh:IDX_DIM + hh + 1], (tq, LANES))
    row = i * tq + lax.broadcasted_iota(I32, (tq, LANES), 0)
    lane = lax.broadcasted_iota(I32, (tq, LANES), 1)

    def score_chunk(c, carry):
        off = pl.multiple_of(c * ts, ts)
        kte = kte_ref[:, pl.ds(off, ts)]
        kto = kto_ref[:, pl.ds(off, ts)]
        acc = [jnp.zeros((tq, LANES), F32) for _ in range(ts // LANES)]
        for p in range(IDX_HEADS // 2):
            lhs = qi_ref[:, p * LANES:(p + 1) * LANES]
            for w_head, rhs in ((wb_ref[2 * p], kte), (wb_ref[2 * p + 1], kto)):
                r = jnp.maximum(_dot(lhs, rhs), 0.0)
                for u in range(ts // LANES):
                    acc[u] = acc[u] + w_head * r[:, u * LANES:(u + 1) * LANES]
        for u in range(ts // LANES):
            col = off + u * LANES + lane
            keys_ref[:, pl.ds(off + u * LANES, LANES)] = jnp.where(col <= row, _score_key(acc[u]), INT_MIN)
        return carry

    lax.fori_loop(0, n_sc, score_chunk, 0)

    n_cc = (n_vis + tc - 1) // tc

    def pad_chunk(c, carry):
        keys_ref[:, pl.ds(pl.multiple_of(c * ts, ts), ts)] = jnp.full((tq, ts), INT_MIN, I32)
        return carry

    lax.fori_loop(n_sc, n_cc * (tc // ts), pad_chunk, 0)

    def count(pred):
        def count_chunk(c, cnt):
            off = pl.multiple_of(c * tc, tc)
            for u in range(tc // LANES):
                cnt = cnt + jnp.where(pred(keys_ref[:, pl.ds(off + u * LANES, LANES)]), 1.0, 0.0)
            return cnt

        cnt = lax.fori_loop(0, n_cc, count_chunk, jnp.zeros((tq, LANES), F32))
        return jnp.sum(cnt, axis=1, keepdims=True)

    thr, cnt_ge = _bisect_threshold(lambda cand: count(lambda k: k >= cand), tq, topk)

    @pl.when(jnp.max(cnt_ge) > topk)
    def _():
        need = topk - count(lambda k: k > thr)
        _demote_excess_ties(keys_ref, thr, need, n_cc * (tc // LANES))

    m_ref[...] = jnp.full(m_ref.shape, NEG_BIG, F32)
    l_ref[...] = jnp.zeros(l_ref.shape, F32)
    acc_ref[...] = jnp.zeros(acc_ref.shape, F32)

    def att_chunk(c, carry):
        off = pl.multiple_of(c * tk, tk)
        n_u = tk // LANES
        bias = jnp.concatenate(
            [jnp.where(keys_ref[:, pl.ds(off + u * LANES, LANES)] >= thr, 0.0, NEG_BIG) for u in range(n_u)], axis=1)
        half = tk // 2
        for hh in range(A_HEADS):
            q = qa_ref[:, hh * HEAD_DIM:(hh + 1) * HEAD_DIM]
            for part in range(2):
                s_ref[hh, :, part * half:(part + 1) * half] = (
                    _nt_dot(q, k_ref[hh, pl.ds(off + part * half, half), :]) + bias[:, part * half:(part + 1) * half])
        for hh in range(A_HEADS):
            m_old = m_ref[hh]
            mx = s_ref[hh, :, 0:LANES]
            for u in range(1, n_u):
                mx = jnp.maximum(mx, s_ref[hh, :, u * LANES:(u + 1) * LANES])
            m_new = jnp.maximum(m_old, jnp.broadcast_to(jnp.max(mx, axis=1, keepdims=True), (tq, LANES)))
            alpha = jnp.exp2(m_old - m_new)
            psum = jnp.zeros((tq, LANES), F32)
            for u in range(n_u):
                p = jnp.exp2(s_ref[hh, :, u * LANES:(u + 1) * LANES] - m_new)
                psum = psum + p
                p_ref[hh, :, u * LANES:(u + 1) * LANES] = p.astype(BF16)
            l_ref[hh] = alpha * l_ref[hh] + jnp.broadcast_to(jnp.sum(psum, axis=1, keepdims=True), (tq, LANES))
            m_ref[hh] = m_new
            a_ref[hh] = alpha
        for hh in range(A_HEADS):
            acc_ref[hh] = a_ref[hh] * acc_ref[hh] + _dot(p_ref[hh], v_ref[hh, pl.ds(off, tk), :])
        return carry

    lax.fori_loop(0, n_ch, att_chunk, 0)
    for hh in range(A_HEADS):
        o_ref[:, hh * HEAD_DIM:(hh + 1) * HEAD_DIM] = (acc_ref[hh] / l_ref[hh]).astype(o_ref.dtype)


def _dsa_prompt(qa, qi, kw, kte, kto, kb, vb):
    t = qa.shape[0]
    tq, ts, tk = 128, 256, 512
    tc = 1024 if t % 1024 == 0 else tk
    assert t % tk == 0 and tc % tk == 0 and tk % ts == 0
    topk = min(TOPK_MAX, t // 4)
    one = pl.Buffered(1)
    head_state = pltpu.VMEM((A_HEADS, tq, LANES), F32)
    return pl.pallas_call(
        functools.partial(_dsa_prompt_kernel, topk=topk, tq=tq, ts=ts, tc=tc, tk=tk),
        grid=(t // tq,),
        in_specs=[pl.BlockSpec((tq, A_WIDTH), lambda i: (i, 0)),
                  pl.BlockSpec((tq, IDX_HEADS * IDX_DIM), lambda i: (i, 0)),
                  pl.BlockSpec((tq, LANES), lambda i: (i, 0)),
                  pl.BlockSpec((LANES, t), lambda i: (0, 0), pipeline_mode=one),
                  pl.BlockSpec((LANES, t), lambda i: (0, 0), pipeline_mode=one),
                  pl.BlockSpec((A_HEADS, t, HEAD_DIM), lambda i: (0, 0, 0), pipeline_mode=one),
                  pl.BlockSpec((A_HEADS, t, HEAD_DIM), lambda i: (0, 0, 0), pipeline_mode=one)],
        out_specs=pl.BlockSpec((tq, A_WIDTH), lambda i: (i, 0)),
        out_shape=jax.ShapeDtypeStruct((t, A_WIDTH), BF16),
        scratch_shapes=[pltpu.VMEM((tq, t), I32), pltpu.VMEM((IDX_HEADS, tq, LANES), F32),
                        head_state, head_state, head_state, head_state,
                        pltpu.VMEM((A_HEADS, tq, tk), F32), pltpu.VMEM((A_HEADS, tq, tk), BF16)],
        compiler_params=_params(1),
        name="dsa_prompt",
    )(qa, qi, kw, kte, kto, kb, vb)


def _dsa_sample_scores_kernel(pt_ref, a_ref, w_ref, *refs, n_tok):
    idx_pages, ki_new, keys_ref = refs[:N_PAGES], refs[N_PAGES], refs[N_PAGES + 1]
    a = a_ref[...]
    wcol = jnp.broadcast_to(w_ref[...], (n_tok * IDX_HEADS, LANES))
    sub8 = lax.broadcasted_iota(I32, (8, LANES), 0)
    lane8 = lax.broadcasted_iota(I32, (8, LANES), 1)
    for j in range(N_PAGES + 1):
        kp = idx_pages[j][...].astype(BF16) if j < N_PAGES else ki_new[...]
        r = jnp.maximum(_dot(a, kp), 0.0) * wcol
        k8 = jnp.full((8, LANES), INT_MIN, I32)
        for t in range(n_tok):
            st = jnp.sum(r[t * IDX_HEADS:(t + 1) * IDX_HEADS, :], axis=0, keepdims=True)
            kt = jnp.broadcast_to(_score_key(st), (8, LANES))
            if j == N_PAGES:
                kt = jnp.where(lane8 <= t, kt, INT_MIN)
            k8 = jnp.where(sub8 == t, kt, k8)
        keys_ref[:, j * LANES:(j + 1) * LANES] = k8


def _threshold_kernel(keys_ref, thr_ref, kout_ref, *, topk):
    rows, n_keys = keys_ref.shape
    kout_ref[...] = keys_ref[...]

    def count(pred):
        cnt = jnp.zeros((rows, LANES), F32)
        for j in range(n_keys // LANES):
            cnt = cnt + jnp.where(pred(keys_ref[:, j * LANES:(j + 1) * LANES]), 1.0, 0.0)
        return jnp.sum(cnt, axis=1, keepdims=True)

    thr, cnt_ge = _bisect_threshold(lambda cand: count(lambda k: k >= cand), rows, topk)
    thr_ref[...] = thr

    @pl.when(jnp.max(cnt_ge) > topk)
    def _():
        need = topk - count(lambda k: k > thr)
        _demote_excess_ties(kout_ref, thr, need, n_keys // LANES)


def _dsa_sample_attend_kernel(pt_ref, q_ref, keys_ref, thr_ref, *refs):
    n_pg = N_PAGES
    k_pages, k_new = refs[:n_pg], refs[n_pg]
    v_pages, v_new = refs[n_pg + 1:2 * n_pg + 1], refs[2 * n_pg + 1]
    o_ref, lg_ref, pn_ref = refs[2 * n_pg + 2:]
    thr = thr_ref[...]
    for j in range(n_pg + 1):
        page = k_pages[j] if j < n_pg else k_new
        bias8 = jnp.where(keys_ref[:, j * LANES:(j + 1) * LANES] >= thr, 0.0, NEG_BIG)
        bias = jnp.concatenate([bias8, bias8], axis=0)
        for hh in range(A_HEADS):
            s = _nt_dot(q_ref[:, hh * HEAD_DIM:(hh + 1) * HEAD_DIM], page[hh].astype(BF16))
            lg_ref[hh, :, j * LANES:(j + 1) * LANES] = s + bias
    for hh in range(A_HEADS):
        lg = lg_ref[hh]
        p = jnp.exp2(lg - jnp.max(lg, axis=1, keepdims=True))
        pn_ref[hh] = (p / jnp.sum(p, axis=1, keepdims=True)).astype(BF16)
    acc = [jnp.zeros((16, HEAD_DIM), F32) for _ in range(A_HEADS)]
    for j in range(n_pg + 1):
        page = v_pages[j] if j < n_pg else v_new
        for hh in range(A_HEADS):
            acc[hh] = acc[hh] + _dot(pn_ref[hh, :, j * LANES:(j + 1) * LANES], page[hh].astype(BF16))
    o_ref[...] = jnp.concatenate(acc, axis=1)


def _dsa_sample(page_table, a, wcol, q16, idx_pool, ki_new, k_pool, k_new, v_pool, v_new, n_tok):
    nb = a.shape[0]
    topk = min(TOPK_MAX, (PAST_LEN + n_tok) // 4)
    n_keys = (N_PAGES + 1) * LANES

    def batch_spec(rows, width):
        return pl.BlockSpec((None, rows, width), lambda b, pt: (b, 0, 0))

    idx_page = lambda j: pl.BlockSpec((None, IDX_DIM, PAGE_SIZE), lambda b, pt, j=j: (pt[b, j], 0, 0))
    keys = pl.pallas_call(
        functools.partial(_dsa_sample_scores_kernel, n_tok=n_tok),
        grid_spec=pltpu.PrefetchScalarGridSpec(
            num_scalar_prefetch=1, grid=(nb,),
            in_specs=[batch_spec(n_tok * IDX_HEADS, IDX_DIM), batch_spec(n_tok * IDX_HEADS, 1)]
            + [idx_page(j) for j in range(N_PAGES)] + [batch_spec(IDX_DIM, PAGE_SIZE)],
            out_specs=batch_spec(8, n_keys)),
        out_shape=jax.ShapeDtypeStruct((nb, 8, n_keys), I32),
        compiler_params=_params(1),
        name="dsa_sample_scores",
    )(page_table, a, wcol, *([idx_pool] * N_PAGES), ki_new)

    rows = nb * 8
    tr = min(rows, 256)
    thr, keys = pl.pallas_call(
        functools.partial(_threshold_kernel, topk=topk),
        grid=(rows // tr,),
        in_specs=[pl.BlockSpec((tr, n_keys), lambda i: (i, 0))],
        out_specs=[pl.BlockSpec((tr, LANES), lambda i: (i, 0)), pl.BlockSpec((tr, n_keys), lambda i: (i, 0))],
        out_shape=[jax.ShapeDtypeStruct((rows, LANES), I32), jax.ShapeDtypeStruct((rows, n_keys), I32)],
        compiler_params=_params(1),
        name="dsa_sample_threshold",
    )(keys.reshape(rows, n_keys))
    thr, keys = thr.reshape(nb, 8, LANES), keys.reshape(nb, 8, n_keys)

    kv_page = lambda j: pl.BlockSpec((None, A_HEADS, PAGE_SIZE, HEAD_DIM), lambda b, pt, j=j: (pt[b, j], 0, 0, 0))
    kv_new = pl.BlockSpec((None, A_HEADS, PAGE_SIZE, HEAD_DIM), lambda b, pt: (b, 0, 0, 0))
    return pl.pallas_call(
        _dsa_sample_attend_kernel,
        grid_spec=pltpu.PrefetchScalarGridSpec(
            num_scalar_prefetch=1, grid=(nb,),
            in_specs=[batch_spec(16, A_WIDTH), batch_spec(8, n_keys), batch_spec(8, LANES)]
            + [kv_page(j) for j in range(N_PAGES)] + [kv_new] + [kv_page(j) for j in range(N_PAGES)] + [kv_new],
            out_specs=batch_spec(16, A_WIDTH),
            scratch_shapes=[pltpu.VMEM((A_HEADS, 16, n_keys), F32), pltpu.VMEM((A_HEADS, 16, n_keys), BF16)]),
        out_shape=jax.ShapeDtypeStruct((nb, 16, A_WIDTH), F32),
        compiler_params=_params(1),
        name="dsa_sample_attend",
    )(page_table, q16, keys, thr, *([k_pool] * N_PAGES), k_new, *([v_pool] * N_PAGES), v_new)


def _ret_consts(chunk):
    lg = jnp.log1p(-jnp.exp2(-5.0 - jnp.arange(R_HEADS, dtype=F32)))
    n = jnp.arange(chunk, dtype=F32)
    diff = n[:, None] - n[None, :]
    dmat = jnp.where(diff >= 0, jnp.exp(lg[:, None, None] * jnp.maximum(diff, 0.0)[None]), 0.0)
    cdec = jnp.exp(lg[:, None] * (n[None, :] + 1.0))
    kdec = jnp.exp(lg[:, None] * (chunk - 1.0 - n[None, :]))
    sdec = jnp.exp(lg * chunk)
    return dmat, cdec, kdec, sdec


def _ret_out_store(o, g, o_ref, sl):
    on = o * lax.rsqrt(jnp.mean(o * o, axis=-1, keepdims=True) + RMS_EPS)
    o_ref[:, sl] = (on * (g * jax.nn.sigmoid(g))).astype(o_ref.dtype)


def _ret_prompt_kernel(q_ref, k_ref, v_ref, g_ref, dmat_ref, cdec_ref, kdec_ref, sdec_ref,
                       o_ref, sout_ref, s_ref):
    c = pl.program_id(0)

    @pl.when(c == 0)
    def _():
        s_ref[...] = jnp.zeros_like(s_ref)

    for hh in range(R_HEADS):
        sl = slice(hh * R_DK, (hh + 1) * R_DK)
        k = k_ref[:, sl]
        qb = q_ref[:, sl].astype(BF16)
        vb = v_ref[:, sl].astype(BF16)
        att = _nt_dot(qb, k.astype(BF16)) * dmat_ref[hh]
        intra = _dot(att.astype(BF16), vb)
        s0 = s_ref[hh]
        cross = _dot(qb, s0.astype(BF16)) * cdec_ref[hh]
        kd_t = (k * kdec_ref[hh]).T.astype(BF16)
        s_ref[hh] = sdec_ref[hh] * s0 + _dot(kd_t, vb)
        _ret_out_store(intra + cross, g_ref[:, sl], o_ref, sl)

    @pl.when(c == pl.num_programs(0) - 1)
    def _():
        sout_ref[...] = s_ref[...]


def _retention_prompt(qr, kr, vg):
    t = qr.shape[0]
    c = RET_CHUNK
    dmat, cdec, kdec, sdec = _ret_consts(c)
    rep = lambda a: jnp.broadcast_to(a[:, :, None], (R_HEADS, c, LANES))
    cdec_b, kdec_b = rep(cdec), rep(kdec)
    sdec_b = jnp.broadcast_to(sdec[:, None, None], (R_HEADS, R_DK, LANES))
    blk = lambda col: pl.BlockSpec((c, R_WIDTH), lambda i, col=col: (i, col))
    const = lambda: pl.BlockSpec((R_HEADS, c, LANES), lambda i: (0, 0, 0))
    return pl.pallas_call(
        _ret_prompt_kernel,
        grid=(t // c,),
        in_specs=[blk(0), blk(0), blk(0), blk(1), const(), const(), const(), const()],
        out_specs=[pl.BlockSpec((c, R_WIDTH), lambda i: (i, 0)),
                   pl.BlockSpec((R_HEADS, R_DK, LANES), lambda i: (0, 0, 0))],
        out_shape=[jax.ShapeDtypeStruct((t, R_WIDTH), BF16),
                   jax.ShapeDtypeStruct((R_HEADS, R_DK, 128), F32)],
        scratch_shapes=[pltpu.VMEM((R_HEADS, R_DK, 128), F32)],
        compiler_params=_params(1),
        name="retention_prompt",
    )(qr, kr, vg, vg, dmat, cdec_b, kdec_b, sdec_b)


def _ret_sample_kernel(q_ref, k_ref, v_ref, g_ref, s0_ref, dmat_ref, cdec_ref, kdec_ref, sdec_ref,
                       o_ref, sout_ref, kpad_ref, vpad_ref):
    @pl.when(pl.program_id(0) == 0)
    def _():
        kpad_ref[...] = jnp.zeros_like(kpad_ref)
        vpad_ref[...] = jnp.zeros_like(vpad_ref)

    kpad_ref[0:8, :] = k_ref[...]
    vpad_ref[0:8, :] = v_ref[...]
    for hh in range(R_HEADS):
        sl = slice(hh * R_DK, (hh + 1) * R_DK)
        kp = kpad_ref[:, sl]
        vb = vpad_ref[:, sl].astype(BF16)
        qb = q_ref[:, sl].astype(BF16)
        att = _nt_dot(qb, kp.astype(BF16)) * dmat_ref[hh]
        intra = _dot(att.astype(BF16), vb)
        s0 = s0_ref[hh]
        cross = _dot(qb, s0.astype(BF16)) * cdec_ref[hh]
        kd_t = (kp * kdec_ref[hh]).T.astype(BF16)
        sout_ref[hh] = sdec_ref[hh] * s0 + _dot(kd_t, vb)
        _ret_out_store(intra + cross, g_ref[:, sl], o_ref, sl)


def _retention_sample(qr, kr, vr, gr, state, n_tok):
    nb = qr.shape[0]
    dmat, cdec, kdec, sdec = _ret_consts(n_tok)
    dmat_p = jnp.zeros((R_HEADS, 8, LANES), F32).at[:, :n_tok, :n_tok].set(dmat)
    cdec_p = jnp.zeros((R_HEADS, 8, LANES), F32).at[:, :n_tok, :].set(
        jnp.broadcast_to(cdec[:, :, None], (R_HEADS, n_tok, LANES)))
    kdec_p = jnp.zeros((R_HEADS, LANES, LANES), F32).at[:, :n_tok, :].set(
        jnp.broadcast_to(kdec[:, :, None], (R_HEADS, n_tok, LANES)))
    sdec_b = jnp.broadcast_to(sdec[:, None, None], (R_HEADS, R_DK, LANES))
    tok = lambda: pl.BlockSpec((None, 8, R_WIDTH), lambda b: (b, 0, 0))
    st = lambda: pl.BlockSpec((None, R_HEADS, R_DK, 128), lambda b: (b, 0, 0, 0))
    const = lambda rows: pl.BlockSpec((R_HEADS, rows, LANES), lambda b: (0, 0, 0))
    return pl.pallas_call(
        _ret_sample_kernel,
        grid=(nb,),
        in_specs=[tok(), tok(), tok(), tok(), st(), const(8), const(8), const(LANES), const(R_DK)],
        out_specs=[tok(), st()],
        out_shape=[jax.ShapeDtypeStruct((nb, 8, R_WIDTH), F32),
                   jax.ShapeDtypeStruct(state.shape, F32)],
        scratch_shapes=[pltpu.VMEM((LANES, R_WIDTH), F32), pltpu.VMEM((LANES, R_WIDTH), F32)],
        compiler_params=_params(1),
        name="retention_sample",
    )(qr, kr, vr, gr, state, dmat_p, cdec_p, kdec_p, sdec_b)


def _mem_attn_prompt_kernel(q_ref, k_ref, v_ref, o_ref):
    for hh in range(M_HEADS):
        sl = slice(hh * HEAD_DIM, (hh + 1) * HEAD_DIM)
        s = _nt_dot(q_ref[:, sl], k_ref[:, sl])
        p = jnp.exp2(s - jnp.max(s, axis=1, keepdims=True))
        pn = (p / jnp.sum(p, axis=1, keepdims=True)).astype(BF16)
        o_ref[:, sl] = _dot(pn, v_ref[:, sl]).astype(o_ref.dtype)


def _mem_attn_prompt(qm, mk, mv, tm):
    t = qm.shape[0]
    nm = mk.shape[0]
    return pl.pallas_call(
        _mem_attn_prompt_kernel,
        grid=(t // tm,),
        in_specs=[pl.BlockSpec((tm, M_WIDTH), lambda i: (i, 0)),
                  pl.BlockSpec((nm, M_WIDTH), lambda i: (0, 0)),
                  pl.BlockSpec((nm, M_WIDTH), lambda i: (0, 0))],
        out_specs=pl.BlockSpec((tm, M_WIDTH), lambda i: (i, 0)),
        out_shape=jax.ShapeDtypeStruct((t, M_WIDTH), BF16),
        compiler_params=_params(1),
        name="mem_attn_prompt",
    )(qm, mk, mv)


def _mem_attn_sample_kernel(q_ref, k_ref, v_ref, o_ref):
    nm = k_ref.shape[1] // M_HEADS
    for g in range(q_ref.shape[0]):
        outs = []
        for hh in range(M_HEADS):
            rows = pl.ds(hh, nm, stride=M_HEADS)
            s = _nt_dot(q_ref[g, :, hh * HEAD_DIM:(hh + 1) * HEAD_DIM], k_ref[g, rows, :].astype(BF16))
            p = jnp.exp2(s - jnp.max(s, axis=1, keepdims=True))
            pn = (p / jnp.sum(p, axis=1, keepdims=True)).astype(BF16)
            outs.append(_dot(pn, v_ref[g, rows, :].astype(BF16)))
        o_ref[g] = jnp.concatenate(outs, axis=1)


def _mem_attn_sample(q16, mem_k, mem_v):
    nb, nm = mem_k.shape[0], mem_k.shape[1]
    gb = 4 if nb % 4 == 0 else 1
    return pl.pallas_call(
        _mem_attn_sample_kernel,
        grid=(nb // gb,),
        in_specs=[pl.BlockSpec((gb, 16, M_WIDTH), lambda b: (b, 0, 0)),
                  pl.BlockSpec((gb, nm, HEAD_DIM), lambda b: (b, 0, 0)),
                  pl.BlockSpec((gb, nm, HEAD_DIM), lambda b: (b, 0, 0))],
        out_specs=pl.BlockSpec((gb, 16, M_WIDTH), lambda b: (b, 0, 0)),
        out_shape=jax.ShapeDtypeStruct((nb, 16, M_WIDTH), F32),
        compiler_params=_params(1),
        name="mem_attn_sample",
    )(q16, mem_k, mem_v)


def _merged_kernel(h_ref, oa_ref, ob_ref, om_ref, wg0_ref, wg1_ref, wg2_ref, wa_ref, wb_ref, wm_ref, o_ref):
    h = h_ref[...]
    acc = None
    for wg, o, wbr in ((wg0_ref, oa_ref, wa_ref), (wg1_ref, ob_ref, wb_ref), (wg2_ref, om_ref, wm_ref)):
        term = jax.nn.sigmoid(_dot(h, wg[...])) * _dot(o[...].astype(BF16), wbr[...])
        acc = term if acc is None else acc + term
    o_ref[...] = acc.astype(o_ref.dtype)


def _merged(h, o_a, o_b, o_m, w_gates, w_a, w_b, w_m, tm, tn):
    m = h.shape[0]
    nj = D_MODEL // tn
    row = lambda width: pl.BlockSpec((tm, width), lambda j, i: (i, 0))
    gate = lambda b: pl.BlockSpec((D_MODEL, tn), lambda j, i, b=b: (0, b * nj + j))
    br = lambda width: pl.BlockSpec((width, tn), lambda j, i: (0, j))
    return pl.pallas_call(
        _merged_kernel,
        grid=(nj, m // tm),
        in_specs=[row(D_MODEL), row(A_WIDTH), row(R_WIDTH), row(M_WIDTH), gate(0), gate(1), gate(2),
                  br(A_WIDTH), br(R_WIDTH), br(M_WIDTH)],
        out_specs=pl.BlockSpec((tm, tn), lambda j, i: (i, j)),
        out_shape=jax.ShapeDtypeStruct((m, D_MODEL), BF16),
        compiler_params=_params(2),
        name="merged",
    )(h, o_a, o_b, o_m, w_gates, w_gates, w_gates, w_a, w_b, w_m)


def _outproj_kernel(x_ref, mg_ref, wo_ref, n2_ref, wrt_hi_ref, wrt_lo_ref, x2_ref, h2_ref, rl_ref):
    x2 = x_ref[...] + _dot(mg_ref[...], wo_ref[...])
    x2_ref[...] = x2
    h2 = x2 * lax.rsqrt(jnp.mean(x2 * x2, axis=-1, keepdims=True) + RMS_EPS) * n2_ref[...]
    h2_hi = h2.astype(BF16)
    h2_ref[...] = h2_hi
    h2_lo = (h2 - h2_hi.astype(F32)).astype(BF16)
    rl_ref[...] = (_dot(h2_hi, wrt_hi_ref[...]) + _dot(h2_hi, wrt_lo_ref[...])) + _dot(h2_lo, wrt_hi_ref[...])


def _outproj(x, merged, w_out, norm2, w_rt, tm):
    m = x.shape[0]
    one = pl.Buffered(1)
    row = lambda width: pl.BlockSpec((tm, width), lambda i: (i, 0))
    w_rt_hi = w_rt.astype(BF16)
    w_rt_lo = (w_rt - w_rt_hi.astype(F32)).astype(BF16)
    return pl.pallas_call(
        _outproj_kernel,
        grid=(m // tm,),
        in_specs=[row(D_MODEL), row(D_MODEL),
                  pl.BlockSpec((D_MODEL, D_MODEL), lambda i: (0, 0), pipeline_mode=one),
                  pl.BlockSpec((1, D_MODEL), lambda i: (0, 0)),
                  pl.BlockSpec((D_MODEL, LANES), lambda i: (0, 0), pipeline_mode=one),
                  pl.BlockSpec((D_MODEL, LANES), lambda i: (0, 0), pipeline_mode=one)],
        out_specs=[row(D_MODEL), row(D_MODEL), row(LANES)],
        out_shape=[jax.ShapeDtypeStruct((m, D_MODEL), F32), jax.ShapeDtypeStruct((m, D_MODEL), BF16),
                   jax.ShapeDtypeStruct((m, LANES), F32)],
        compiler_params=_params(1),
        name="outproj",
    )(x, merged, w_out, norm2.reshape(1, D_MODEL), w_rt_hi, w_rt_lo)


def _router_kernel(rl_ref, comb_ref):
    z = rl_ref[...]
    lane = lax.broadcasted_iota(I32, z.shape, 1).astype(F32)
    ninf = -jnp.inf
    big = 1e9
    is_g = lane < N_GROUPS
    gl = jnp.where(is_g, z, ninf)
    gmax = jnp.max(gl, axis=1, keepdims=True)
    gsum = jnp.sum(jnp.where(is_g, jnp.exp(gl - gmax), 0.0), axis=1, keepdims=True)
    p_sel = 1.0 / gsum
    g_sel = jnp.min(jnp.where(gl == gmax, lane, big), axis=1, keepdims=True)
    lo = N_GROUPS + EXP_PER_GROUP * g_sel
    el = jnp.where(lane >= lo, jnp.where(lane < lo + EXP_PER_GROUP, z, ninf), ninf)
    v1 = jnp.max(el, axis=1, keepdims=True)
    i1 = jnp.min(jnp.where(el == v1, lane, big), axis=1, keepdims=True)
    el2 = jnp.where(lane == i1, ninf, el)
    v2 = jnp.max(el2, axis=1, keepdims=True)
    i2 = jnp.min(jnp.where(el2 == v2, lane, big), axis=1, keepdims=True)
    e2 = jnp.exp(v2 - v1)
    w1 = 1.0 / (1.0 + e2)
    w2 = e2 / (1.0 + e2)
    comb_ref[...] = jnp.where(lane == i1, w1 * p_sel, 0.0) + jnp.where(lane == i2, w2 * p_sel, 0.0)


def _router(rl, tm):
    m = rl.shape[0]
    return pl.pallas_call(
        _router_kernel,
        grid=(m // tm,),
        in_specs=[pl.BlockSpec((tm, LANES), lambda i: (i, 0))],
        out_specs=pl.BlockSpec((tm, LANES), lambda i: (i, 0)),
        out_shape=jax.ShapeDtypeStruct((m, LANES), F32),
        compiler_params=_params(1),
        name="router",
    )(rl)


def _moe_kernel(x2_ref, h2_ref, comb_ref, wg_ref, wu_ref, wd_ref, y_ref, acc_ref, *, ne):
    k = pl.program_id(1)

    @pl.when(k == 0)
    def _():
        acc_ref[...] = x2_ref[...]

    h2 = h2_ref[...]
    comb = comb_ref[...]
    lane = lax.broadcasted_iota(I32, comb.shape, 1)
    for e in range(ne):
        ce = jnp.sum(jnp.where(lane == N_GROUPS + k * ne + e, comb, 0.0), axis=1, keepdims=True)
        a = _dot(h2, wg_ref[e])
        u = _dot(h2, wu_ref[e])
        act = (a * jax.nn.sigmoid(a)) * u * ce
        acc_ref[...] += _dot(act.astype(BF16), wd_ref[e])

    @pl.when(k == pl.num_programs(1) - 1)
    def _():
        y_ref[...] = acc_ref[...]


def _moe(x2, h2, comb, wg, wu, wd, tm, ne):
    m = x2.shape[0]
    row = lambda width: pl.BlockSpec((tm, width), lambda i, k: (i, 0))
    return pl.pallas_call(
        functools.partial(_moe_kernel, ne=ne),
        grid=(m // tm, N_EXPERTS // ne),
        in_specs=[row(D_MODEL), row(D_MODEL), row(LANES),
                  pl.BlockSpec((ne, D_MODEL, EXP_FF), lambda i, k: (k, 0, 0)),
                  pl.BlockSpec((ne, D_MODEL, EXP_FF), lambda i, k: (k, 0, 0)),
                  pl.BlockSpec((ne, EXP_FF, D_MODEL), lambda i, k: (k, 0, 0))],
        out_specs=row(D_MODEL),
        out_shape=jax.ShapeDtypeStruct((m, D_MODEL), F32),
        scratch_shapes=[pltpu.VMEM((tm, D_MODEL), F32)],
        compiler_params=_params(2),
        name="moe",
    )(x2, h2, comb, wg, wu, wd)


def _rope_tables(pos):
    half = R_DK // 2
    freqs = ROPE_BASE ** (-jnp.arange(half, dtype=F32) / half)
    ang = pos.astype(F32)[:, None] * freqs[None, :]
    cos, sin = jnp.cos(ang), jnp.sin(ang)
    return jnp.concatenate([cos, cos], axis=1), jnp.concatenate([-sin, sin], axis=1)


def _layer_weights(l, w_in, w_mem_kv, w_br_a, w_br_b, w_br_m, w_out, w_group, w_router,
                   w_gate_e, w_up_e, w_down_e):
    wi = w_in[l]
    col = lambda a, b: wi[:, a:b].astype(BF16)
    w_kw = jnp.pad(wi[:, O_KI:O_QR], ((0, 0), (0, LANES - (O_QR - O_KI)))).astype(BF16)
    w_rt = jnp.pad(jnp.concatenate([w_group[l], w_router[l]], axis=1),
                   ((0, 0), (0, LANES - N_GROUPS - N_EXPERTS)))
    return dict(
        qa=col(O_QA, O_KA), ka=col(O_KA, O_VA), va=col(O_VA, O_QI), qi=col(O_QI, O_KI), kw=w_kw,
        qr=col(O_QR, O_KR), kr=col(O_KR, O_VR), vg=col(O_VR, O_QM), qm=col(O_QM, O_GATES),
        gates=col(O_GATES, O_END),
        mk=w_mem_kv[l][:, :M_WIDTH].astype(BF16), mv=w_mem_kv[l][:, M_WIDTH:].astype(BF16),
        br_a=w_br_a[l].astype(BF16), br_b=w_br_b[l].astype(BF16), br_m=w_br_m[l].astype(BF16),
        out=w_out[l].astype(BF16), rt=w_rt,
        ge=w_gate_e[l].astype(BF16), ue=w_up_e[l].astype(BF16), de=w_down_e[l].astype(BF16))


def _project(x2d, pos, w, norms, tm):
    t = x2d.shape[0]
    norm1, q_norm_a, k_norm_a, idx_k_norm, q_norm_m = norms
    h = _rmsnorm_cast(x2d, norm1, tm)
    hd = lambda g: g.reshape(1, HEAD_DIM)
    gspec = _const_spec((1, HEAD_DIM))
    scale = HEAD_DIM ** -0.5 * LOG2_E
    (qa,) = _proj(functools.partial(_proj_headnorm_kernel, scale=scale), h, w["qa"], [hd(q_norm_a)], [gspec],
                  [(BF16, ROW)], tm, A_WIDTH, "proj_qa")
    ka, kab = _proj(functools.partial(_proj_headnorm_kernel, scale=1.0), h, w["ka"], [hd(k_norm_a)], [gspec],
                    [(F32, HEAD_MAJOR), (BF16, HEAD_MAJOR)], tm, A_WIDTH, "proj_ka")
    va, vab = _proj(_proj_plain_kernel, h, w["va"], [], [], [(F32, HEAD_MAJOR), (BF16, HEAD_MAJOR)], tm, A_WIDTH,
                    "proj_va")
    (qi,) = _proj(_proj_plain_kernel, h, w["qi"], [], [], [(BF16, ROW)], tm, 512, "proj_qi")
    g_idx = jnp.pad(idx_k_norm, (0, LANES - IDX_DIM)).reshape(1, LANES)
    kw, kt, kte, kto = _proj(_proj_kw_kernel, h, w["kw"], [g_idx], [_const_spec((1, LANES))],
                             [(F32, ROW), (F32, TRANSPOSED), (BF16, TRANSPOSED), (BF16, TRANSPOSED)], tm, LANES,
                             "proj_kw")
    cos2, sin2 = _rope_tables(pos)
    tabs, tab_specs = [cos2, sin2], [_row_spec(tm, R_DK), _row_spec(tm, R_DK)]
    (qr,) = _proj(functools.partial(_proj_rope_kernel, scale=1.0), h, w["qr"], tabs, tab_specs,
                  [(F32, ROW)], tm, R_WIDTH, "proj_qr")
    (kr,) = _proj(functools.partial(_proj_rope_kernel, scale=R_DK ** -0.5), h, w["kr"], tabs, tab_specs,
                  [(F32, ROW)], tm, R_WIDTH, "proj_kr")
    (vg,) = _proj(_proj_plain_kernel, h, w["vg"], [], [], [(F32, ROW)], tm, R_WIDTH, "proj_vg")
    (qm,) = _proj(functools.partial(_proj_headnorm_kernel, scale=scale), h, w["qm"], [hd(q_norm_m)], [gspec],
                  [(BF16, ROW)], tm, M_WIDTH, "proj_qm")
    return dict(h=h, qa=qa, ka=ka, kab=kab, va=va, vab=vab, qi=qi, kw=kw, kt=kt, kte=kte, kto=kto,
                qr=qr, kr=kr, vg=vg, qm=qm)


def _finish(x2d, h, o_a, o_b, o_m, w, norm2, tm):
    merged = _merged(h, o_a, o_b, o_m, w["gates"], w["br_a"], w["br_b"], w["br_m"], tm, 512)
    x2, h2, rl = _outproj(x2d, merged, w["out"], norm2, w["rt"], min(tm, 256))
    comb = _router(rl, tm)
    return _moe(x2, h2, comb, w["ge"], w["ue"], w["de"], tm, 2)


def kernel(x_prompt, x_sample, mem_prompt, cache_k, cache_v, cache_idx_k, state_ret, cache_mem_k, cache_mem_v,
           page_table, norm1, w_in, q_norm_a, k_norm_a, idx_k_norm, q_norm_m, k_norm_m, mem_norm, w_mem_kv,
           w_br_a, w_br_b, w_br_m, w_out, norm2, w_group, w_router, w_gate_e, w_up_e, w_down_e):
    depth = w_in.shape[0]
    bp, seq, _ = x_prompt.shape
    nb, n_tok, _ = x_sample.shape
    assert bp == 1 and n_tok <= 8
    y_p = x_prompt.reshape(seq, D_MODEL)
    y_s = x_sample.reshape(nb * n_tok, D_MODEL)
    pos_p = jnp.arange(seq)
    pos_s = jnp.tile(PAST_LEN + jnp.arange(n_tok), nb)
    outs = [[] for _ in range(10)]
    for l in range(depth):
        w = _layer_weights(l, w_in, w_mem_kv, w_br_a, w_br_b, w_br_m, w_out, w_group, w_router,
                           w_gate_e, w_up_e, w_down_e)
        norms = (norm1[l], q_norm_a[l], k_norm_a[l], idx_k_norm[l], q_norm_m[l])

        tm = 512
        p = _project(y_p, pos_p, w, norms, tm)
        o_a = _dsa_prompt(p["qa"], p["qi"], p["kw"], p["kte"], p["kto"], p["kab"], p["vab"])
        o_r, s_fin = _retention_prompt(p["qr"], p["kr"], p["vg"])
        n_mem = mem_prompt.shape[1]
        hm = _rmsnorm_cast(mem_prompt.reshape(n_mem, D_MODEL), mem_norm[l], n_mem)
        mk, mkb = _proj(functools.partial(_proj_headnorm_kernel, scale=1.0), hm, w["mk"],
                        [k_norm_m[l].reshape(1, HEAD_DIM)], [_const_spec((1, HEAD_DIM))],
                        [(F32, ROW), (BF16, ROW)], n_mem, M_WIDTH, "proj_mk")
        mv, mvb = _proj(_proj_plain_kernel, hm, w["mv"], [], [], [(F32, ROW), (BF16, ROW)],
                        n_mem, M_WIDTH, "proj_mv")
        o_m = _mem_attn_prompt(p["qm"], mkb, mvb, tm)
        y_p = _finish(y_p, p["h"], o_a, o_r, o_m, w, norm2[l], tm)
        outs[0].append(jnp.transpose(p["ka"], (1, 0, 2)).reshape(bp, seq, A_HEADS, HEAD_DIM))
        outs[1].append(jnp.transpose(p["va"], (1, 0, 2)).reshape(bp, seq, A_HEADS, HEAD_DIM))
        outs[2].append(p["kt"][:IDX_DIM].T.reshape(bp, seq, IDX_DIM))
        outs[3].append(s_fin.reshape(bp, R_HEADS, R_DK, 128))
        outs[4].append(mk.reshape(bp, n_mem, M_HEADS, HEAD_DIM))
        outs[5].append(mv.reshape(bp, n_mem, M_HEADS, HEAD_DIM))

        ts = nb * n_tok
        s = _project(y_s, pos_s, w, norms, ts)
        pad_tok = lambda a: jnp.pad(a.reshape(nb, n_tok, -1), ((0, 0), (0, 8 - n_tok), (0, 0)))
        a_idx = s["qi"].reshape(nb, n_tok * IDX_HEADS, IDX_DIM)
        w_idx = s["kw"][:, IDX_DIM:IDX_DIM + IDX_HEADS].reshape(nb, n_tok * IDX_HEADS, 1)
        pad16 = lambda a: jnp.pad(a.reshape(nb, n_tok, -1), ((0, 0), (0, 16 - n_tok), (0, 0)))
        ki_new = jnp.pad(jnp.transpose(s["kte"][:IDX_DIM].reshape(IDX_DIM, nb, n_tok), (1, 0, 2)),
                         ((0, 0), (0, 0), (0, PAGE_SIZE - n_tok)))
        new_page = lambda a: jnp.pad(jnp.transpose(a.reshape(A_HEADS, nb, n_tok, HEAD_DIM), (1, 0, 2, 3)),
                                     ((0, 0), (0, 0), (0, PAGE_SIZE - n_tok), (0, 0)))
        head_major_pool = lambda c: jnp.transpose(c, (0, 2, 1, 3))
        o_a_s = _dsa_sample(page_table, a_idx, w_idx, pad16(s["qa"]),
                            jnp.swapaxes(cache_idx_k[l], 1, 2), ki_new,
                            head_major_pool(cache_k[l]), new_page(s["kab"]),
                            head_major_pool(cache_v[l]), new_page(s["vab"]), n_tok)
        o_r_s, s_new = _retention_sample(pad_tok(s["qr"]), pad_tok(s["kr"]), pad_tok(s["vg"][:, :R_WIDTH]),
                                         pad_tok(s["vg"][:, R_WIDTH:]), state_ret[l], n_tok)
        mem_rows = lambda c: c.reshape(nb, -1, HEAD_DIM)
        o_m_s = _mem_attn_sample(pad16(s["qm"]), mem_rows(cache_mem_k[l]), mem_rows(cache_mem_v[l]))
        unpad = lambda a: a[:, :n_tok, :].reshape(ts, -1)
        y_s = _finish(y_s, s["h"], unpad(o_a_s), unpad(o_r_s), unpad(o_m_s), w, norm2[l], ts)
        outs[6].append(jnp.transpose(s["ka"], (1, 0, 2)).reshape(nb, n_tok, A_HEADS, HEAD_DIM))
        outs[7].append(jnp.transpose(s["va"], (1, 0, 2)).reshape(nb, n_tok, A_HEADS, HEAD_DIM))
        outs[8].append(s["kt"][:IDX_DIM].T.reshape(nb, n_tok, IDX_DIM))
        outs[9].append(s_new)
    stk = [jnp.stack(o) for o in outs]
    return (y_p.reshape(bp, seq, D_MODEL), y_s.reshape(nb, n_tok, D_MODEL), *stk)
```

```python
import functools

import jax
import jax.numpy as jnp
import numpy as np
from jax import lax
from jax.experimental import pallas as pl
from jax.experimental.pallas import tpu as pltpu

F32 = jnp.float32
BF16 = jnp.bfloat16
I32 = jnp.int32

D_MODEL = 2048
PAST_LEN = 2048
PAGE_SIZE = 128
N_PAGES = PAST_LEN // PAGE_SIZE
HEAD_DIM = 128
A_HEADS = 6
A_WIDTH = A_HEADS * HEAD_DIM
IDX_HEADS = 16
IDX_DIM = 64
TOPK_MAX = 256
R_HEADS = 6
R_DK = 128
R_WIDTH = R_HEADS * 128
RET_CHUNK = 128
ROPE_BASE = 10000.0
M_HEADS = 4
M_WIDTH = M_HEADS * HEAD_DIM
N_BRANCH = 3
N_GROUPS = 4
EXP_PER_GROUP = 8
N_EXPERTS = N_GROUPS * EXP_PER_GROUP
EXP_FF = 256
RMS_EPS = 1e-6

LANES = 128
INT_MIN = -(2 ** 31)
NEG_BIG = -1e30
LOG2_E = 1.4426950408889634
V7X_VMEM_LIMIT = 56 * 1024 * 1024

_SIZES = (A_WIDTH, A_WIDTH, A_WIDTH, IDX_HEADS * IDX_DIM, IDX_DIM, IDX_HEADS,
          R_HEADS * R_DK, R_HEADS * R_DK, R_WIDTH, R_WIDTH, M_WIDTH, N_BRANCH * D_MODEL)
_OFFS = tuple(int(v) for v in np.cumsum((0,) + _SIZES))
(O_QA, O_KA, O_VA, O_QI, O_KI, O_WI, O_QR, O_KR, O_VR, O_GR, O_QM, O_GATES, O_END) = _OFFS


def _params(n_grid, vmem=V7X_VMEM_LIMIT):
    return pltpu.CompilerParams(dimension_semantics=("arbitrary",) * n_grid, vmem_limit_bytes=vmem)


def _nt_dot(a, b):
    return lax.dot_general(a, b, (((1,), (1,)), ((), ())), preferred_element_type=F32)


def _dot(a, b):
    return jnp.dot(a, b, preferred_element_type=F32)


def _rmsnorm_kernel(x_ref, g_ref, o_ref):
    x = x_ref[...]
    ms = jnp.mean(x * x, axis=-1, keepdims=True)
    o_ref[...] = (x * lax.rsqrt(ms + RMS_EPS) * g_ref[...]).astype(o_ref.dtype)


def _rmsnorm_cast(x, g, tm):
    m, d = x.shape
    return pl.pallas_call(
        _rmsnorm_kernel,
        grid=(m // tm,),
        in_specs=[pl.BlockSpec((tm, d), lambda i: (i, 0)), pl.BlockSpec((1, d), lambda i: (0, 0))],
        out_specs=pl.BlockSpec((tm, d), lambda i: (i, 0)),
        out_shape=jax.ShapeDtypeStruct((m, d), BF16),
        compiler_params=_params(1),
        name="rmsnorm_cast",
    )(x, g.reshape(1, d))


def _store_cols(o_ref, hh, z):
    if len(o_ref.shape) == 3:
        o_ref[hh] = z.astype(o_ref.dtype)
    else:
        o_ref[:, hh * HEAD_DIM:(hh + 1) * HEAD_DIM] = z.astype(o_ref.dtype)


def _proj_plain_kernel(h_ref, w_ref, *o_refs):
    y = _dot(h_ref[...], w_ref[...])
    for o in o_refs:
        if len(o.shape) == 3:
            for hh in range(o.shape[0]):
                o[hh] = y[:, hh * HEAD_DIM:(hh + 1) * HEAD_DIM].astype(o.dtype)
        else:
            o[...] = y.astype(o.dtype)


def _proj_headnorm_kernel(h_ref, w_ref, g_ref, *o_refs, scale):
    y = _dot(h_ref[...], w_ref[...])
    g = g_ref[...]
    for hh in range(y.shape[1] // HEAD_DIM):
        sl = slice(hh * HEAD_DIM, (hh + 1) * HEAD_DIM)
        yh = y[:, sl]
        ms = jnp.mean(yh * yh, axis=-1, keepdims=True)
        z = yh * lax.rsqrt(ms + RMS_EPS) * g
        if scale != 1.0:
            z = z * scale
        for o in o_refs:
            _store_cols(o, hh, z)


def _proj_rope_kernel(h_ref, w_ref, cos_ref, sin_ref, o_ref, *, scale):
    y = _dot(h_ref[...], w_ref[...])
    c = cos_ref[...]
    s = sin_ref[...]
    for hh in range(y.shape[1] // R_DK):
        sl = slice(hh * R_DK, (hh + 1) * R_DK)
        yh = y[:, sl]
        z = yh * c + pltpu.roll(yh, R_DK // 2, 1) * s
        if scale != 1.0:
            z = z * scale
        o_ref[:, sl] = z


def _proj_kw_kernel(h_ref, w_ref, g_ref, kw_ref, kt_ref, kte_ref, kto_ref):
    y = _dot(h_ref[...], w_ref[...])
    lane = lax.broadcasted_iota(I32, y.shape, 1)
    is_k = lane < IDX_DIM
    ms = jnp.sum(jnp.where(is_k, y * y, 0.0), axis=-1, keepdims=True) * (1.0 / IDX_DIM)
    kn = y * lax.rsqrt(ms + RMS_EPS) * g_ref[...]
    kw_ref[...] = jnp.where(is_k, kn, y * (IDX_HEADS * IDX_DIM) ** -0.5)
    et = jnp.where(is_k, kn, 0.0).T
    kt_ref[...] = et
    kte_ref[...] = et.astype(BF16)
    kto_ref[...] = pltpu.roll(et, IDX_DIM, 0).astype(BF16)


ROW, TRANSPOSED, HEAD_MAJOR = "row", "transposed", "head_major"


def _proj(kern, h, w, extras, extra_specs, outs, tm, tn, name):
    m, k = h.shape
    n = w.shape[1]
    out_shape, out_specs = [], []
    for dt, layout in outs:
        if layout == TRANSPOSED:
            out_shape.append(jax.ShapeDtypeStruct((n, m), dt))
            out_specs.append(pl.BlockSpec((tn, tm), lambda j, i: (j, i)))
        elif layout == HEAD_MAJOR:
            out_shape.append(jax.ShapeDtypeStruct((n // HEAD_DIM, m, HEAD_DIM), dt))
            out_specs.append(pl.BlockSpec((tn // HEAD_DIM, tm, HEAD_DIM), lambda j, i: (j, i, 0)))
        else:
            out_shape.append(jax.ShapeDtypeStruct((m, n), dt))
            out_specs.append(pl.BlockSpec((tm, tn), lambda j, i: (i, j)))
    res = pl.pallas_call(
        kern,
        grid=(n // tn, m // tm),
        in_specs=[pl.BlockSpec((tm, k), lambda j, i: (i, 0)),
                  pl.BlockSpec((k, tn), lambda j, i: (0, j))] + list(extra_specs),
        out_specs=out_specs,
        out_shape=out_shape,
        compiler_params=_params(2),
        name=name,
    )(h, w, *extras)
    return res


def _row_spec(tm, width):
    return pl.BlockSpec((tm, width), lambda j, i: (i, 0))


def _const_spec(shape):
    return pl.BlockSpec(shape, lambda j, i: (0,) * len(shape))


def _score_key(score):
    bits = lax.bitcast_convert_type(score, I32)
    return jnp.where(bits < 0, bits ^ jnp.int32(0x7FFFFFFF), bits)


def _bisect_threshold(count_ge, rows, topk):
    def bit_body(it, carry):
        ua, cnt = carry
        cand_u = ua | jnp.left_shift(jnp.int32(1), 31 - it)
        total = count_ge(cand_u ^ INT_MIN)
        ok = total >= topk
        return jnp.where(ok, cand_u, ua), jnp.where(ok, jnp.broadcast_to(total, cnt.shape), cnt)

    init = (jnp.zeros((rows, LANES), I32), jnp.zeros((rows, LANES), F32))
    ua, cnt = lax.fori_loop(0, 32, bit_body, init)
    return jnp.maximum(ua ^ INT_MIN, INT_MIN + 1), cnt


def _demote_excess_ties(keys_ref, thr, need, n_pieces):
    rows = keys_ref.shape[0]
    tri = jnp.where(lax.broadcasted_iota(I32, (LANES, LANES), 0) <= lax.broadcasted_iota(I32, (LANES, LANES), 1),
                    1.0, 0.0).astype(BF16)

    def piece(j, seen):
        off = pl.multiple_of(j * LANES, LANES)
        k = keys_ref[:, pl.ds(off, LANES)]
        tie = jnp.where(k == thr, 1.0, 0.0)
        rank = seen + _dot(tie.astype(BF16), tri)
        keys_ref[:, pl.ds(off, LANES)] = jnp.where(tie * rank > need, INT_MIN, k)
        return jnp.broadcast_to(rank[:, LANES - 1:LANES], (rows, LANES))

    lax.fori_loop(0, n_pieces, piece, jnp.zeros((rows, LANES), F32))


def _dsa_prompt_kernel(qa_ref, qi_ref, kw_ref, kte_ref, kto_ref, k_ref, v_ref, o_ref,
                       keys_ref, wb_ref, m_ref, l_ref, acc_ref, a_ref, s_ref, p_ref, *, topk, tq, ts, tc, tk):
    i = pl.program_id(0)
    n_vis = i * tq + tq
    n_sc = (n_vis + ts - 1) // ts
    n_ch = (n_vis + tk - 1) // tk
    kw = kw_ref[...]
    for hh in range(IDX_HEADS):
        wb_ref[hh] = jnp.broadcast_to(kw[:, IDX_DIM + hh:IDX_DIM + hh + 1], (tq, LANES))
    row = i * tq + lax.broadcasted_iota(I32, (tq, LANES), 0)
    lane = lax.broadcasted_iota(I32, (tq, LANES), 1)

    def score_chunk(c, carry):
        off = pl.multiple_of(c * ts, ts)
        kte = kte_ref[:, pl.ds(off, ts)]
        kto = kto_ref[:, pl.ds(off, ts)]
        acc = [jnp.zeros((tq, LANES), F32) for _ in range(ts // LANES)]
        for p in range(IDX_HEADS // 2):
            lhs = qi_ref[:, p * LANES:(p + 1) * LANES]
            for w_head, rhs in ((wb_ref[2 * p], kte), (wb_ref[2 * p + 1], kto)):
                r = jnp.maximum(_dot(lhs, rhs), 0.0)
                for u in range(ts // LANES):
                    acc[u] = acc[u] + w_head * r[:, u * LANES:(u + 1) * LANES]
        for u in range(ts // LANES):
            col = off + u * LANES + lane
            keys_ref[:, pl.ds(off + u * LANES, LANES)] = jnp.where(col <= row, _score_key(acc[u]), INT_MIN)
        return carry

    lax.fori_loop(0, n_sc, score_chunk, 0)

    n_cc = (n_vis + tc - 1) // tc

    def pad_chunk(c, carry):
        keys_ref[:, pl.ds(pl.multiple_of(c * ts, ts), ts)] = jnp.full((tq, ts), INT_MIN, I32)
        return carry

    lax.fori_loop(n_sc, n_cc * (tc // ts), pad_chunk, 0)

    def count(pred):
        def count_chunk(c, cnt):
            off = pl.multiple_of(c * tc, tc)
            for u in range(tc // LANES):
                cnt = cnt + jnp.where(pred(keys_ref[:, pl.ds(off + u * LANES, LANES)]), 1.0, 0.0)
            return cnt

        cnt = lax.fori_loop(0, n_cc, count_chunk, jnp.zeros((tq, LANES), F32))
        return jnp.sum(cnt, axis=1, keepdims=True)

    thr, cnt_ge = _bisect_threshold(lambda cand: count(lambda k: k >= cand), tq, topk)

    @pl.when(jnp.max(cnt_ge) > topk)
    def _():
        need = topk - count(lambda k: k > thr)
        _demote_excess_ties(keys_ref, thr, need, n_cc * (tc // LANES))

    m_ref[...] = jnp.full(m_ref.shape, NEG_BIG, F32)
    l_ref[...] = jnp.zeros(l_ref.shape, F32)
    acc_ref[...] = jnp.zeros(acc_ref.shape, F32)

    def att_chunk(c, carry):
        off = pl.multiple_of(c * tk, tk)
        n_u = tk // LANES
        bias = jnp.concatenate(
            [jnp.where(keys_ref[:, pl.ds(off + u * LANES, LANES)] >= thr, 0.0, NEG_BIG) for u in range(n_u)], axis=1)
        half = tk // 2
        for hh in range(A_HEADS):
            q = qa_ref[:, hh * HEAD_DIM:(hh + 1) * HEAD_DIM]
            for part in range(2):
                s_ref[hh, :, part * half:(part + 1) * half] = (
                    _nt_dot(q, k_ref[hh, pl.ds(off + part * half, half), :]) + bias[:, part * half:(part + 1) * half])
        for hh in range(A_HEADS):
            m_old = m_ref[hh]
            mx = s_ref[hh, :, 0:LANES]
            for u in range(1, n_u):
                mx = jnp.maximum(mx, s_ref[hh, :, u * LANES:(u + 1) * LANES])
            m_new = jnp.maximum(m_old, jnp.broadcast_to(jnp.max(mx, axis=1, keepdims=True), (tq, LANES)))
            alpha = jnp.exp2(m_old - m_new)
            psum = jnp.zeros((tq, LANES), F32)
            for u in range(n_u):
                p = jnp.exp2(s_ref[hh, :, u * LANES:(u + 1) * LANES] - m_new)
                psum = psum + p
                p_ref[hh, :, u * LANES:(u + 1) * LANES] = p.astype(BF16)
            l_ref[hh] = alpha * l_ref[hh] + jnp.broadcast_to(jnp.sum(psum, axis=1, keepdims=True), (tq, LANES))
            m_ref[hh] = m_new
            a_ref[hh] = alpha
        for hh in range(A_HEADS):
            acc_ref[hh] = a_ref[hh] * acc_ref[hh] + _dot(p_ref[hh], v_ref[hh, pl.ds(off, tk), :])
        return carry

    lax.fori_loop(0, n_ch, att_chunk, 0)
    for hh in range(A_HEADS):
        o_ref[:, hh * HEAD_DIM:(hh + 1) * HEAD_DIM] = (acc_ref[hh] / l_ref[hh]).astype(o_ref.dtype)


def _dsa_prompt(qa, qi, kw, kte, kto, kb, vb):
    t = qa.shape[0]
    tq, ts, tk = 128, 256, 512
    tc = 1024 if t % 1024 == 0 else tk
    assert t % tk == 0 and tc % tk == 0 and tk % ts == 0
    topk = min(TOPK_MAX, t // 4)
    one = pl.Buffered(1)
    head_state = pltpu.VMEM((A_HEADS, tq, LANES), F32)
    return pl.pallas_call(
        functools.partial(_dsa_prompt_kernel, topk=topk, tq=tq, ts=ts, tc=tc, tk=tk),
        grid=(t // tq,),
        in_specs=[pl.BlockSpec((tq, A_WIDTH), lambda i: (i, 0)),
                  pl.BlockSpec((tq, IDX_HEADS * IDX_DIM), lambda i: (i, 0)),
                  pl.BlockSpec((tq, LANES), lambda i: (i, 0)),
                  pl.BlockSpec((LANES, t), lambda i: (0, 0), pipeline_mode=one),
                  pl.BlockSpec((LANES, t), lambda i: (0, 0), pipeline_mode=one),
                  pl.BlockSpec((A_HEADS, t, HEAD_DIM), lambda i: (0, 0, 0), pipeline_mode=one),
                  pl.BlockSpec((A_HEADS, t, HEAD_DIM), lambda i: (0, 0, 0), pipeline_mode=one)],
        out_specs=pl.BlockSpec((tq, A_WIDTH), lambda i: (i, 0)),
        out_shape=jax.ShapeDtypeStruct((t, A_WIDTH), BF16),
        scratch_shapes=[pltpu.VMEM((tq, t), I32), pltpu.VMEM((IDX_HEADS, tq, LANES), F32),
                        head_state, head_state, head_state, head_state,
                        pltpu.VMEM((A_HEADS, tq, tk), F32), pltpu.VMEM((A_HEADS, tq, tk), BF16)],
        compiler_params=_params(1),
        name="dsa_prompt",
    )(qa, qi, kw, kte, kto, kb, vb)


def _dsa_sample_scores_kernel(pt_ref, a_ref, w_ref, *refs, n_tok):
    idx_pages, ki_new, keys_ref = refs[:N_PAGES], refs[N_PAGES], refs[N_PAGES + 1]
    a = a_ref[...]
    wcol = jnp.broadcast_to(w_ref[...], (n_tok * IDX_HEADS, LANES))
    sub8 = lax.broadcasted_iota(I32, (8, LANES), 0)
    lane8 = lax.broadcasted_iota(I32, (8, LANES), 1)
    for j in range(N_PAGES + 1):
        kp = idx_pages[j][...].astype(BF16) if j < N_PAGES else ki_new[...]
        r = jnp.maximum(_dot(a, kp), 0.0) * wcol
        k8 = jnp.full((8, LANES), INT_MIN, I32)
        for t in range(n_tok):
            st = jnp.sum(r[t * IDX_HEADS:(t + 1) * IDX_HEADS, :], axis=0, keepdims=True)
            kt = jnp.broadcast_to(_score_key(st), (8, LANES))
            if j == N_PAGES:
                kt = jnp.where(lane8 <= t, kt, INT_MIN)
            k8 = jnp.where(sub8 == t, kt, k8)
        keys_ref[:, j * LANES:(j + 1) * LANES] = k8


def _threshold_kernel(keys_ref, thr_ref, kout_ref, *, topk):
    rows, n_keys = keys_ref.shape
    kout_ref[...] = keys_ref[...]

    def count(pred):
        cnt = jnp.zeros((rows, LANES), F32)
        for j in range(n_keys // LANES):
            cnt = cnt + jnp.where(pred(keys_ref[:, j * LANES:(j + 1) * LANES]), 1.0, 0.0)
        return jnp.sum(cnt, axis=1, keepdims=True)

    thr, cnt_ge = _bisect_threshold(lambda cand: count(lambda k: k >= cand), rows, topk)
    thr_ref[...] = thr

    @pl.when(jnp.max(cnt_ge) > topk)
    def _():
        need = topk - count(lambda k: k > thr)
        _demote_excess_ties(kout_ref, thr, need, n_keys // LANES)


def _dsa_sample_attend_kernel(pt_ref, q_ref, keys_ref, thr_ref, *refs):
    n_pg = N_PAGES
    k_pages, k_new = refs[:n_pg], refs[n_pg]
    v_pages, v_new = refs[n_pg + 1:2 * n_pg + 1], refs[2 * n_pg + 1]
    o_ref, lg_ref, pn_ref = refs[2 * n_pg + 2:]
    thr = thr_ref[...]
    for j in range(n_pg + 1):
        page = k_pages[j] if j < n_pg else k_new
        bias8 = jnp.where(keys_ref[:, j * LANES:(j + 1) * LANES] >= thr, 0.0, NEG_BIG)
        bias = jnp.concatenate([bias8, bias8], axis=0)
        for hh in range(A_HEADS):
            s = _nt_dot(q_ref[:, hh * HEAD_DIM:(hh + 1) * HEAD_DIM], page[hh].astype(BF16))
            lg_ref[hh, :, j * LANES:(j + 1) * LANES] = s + bias
    for hh in range(A_HEADS):
        lg = lg_ref[hh]
        p = jnp.exp2(lg - jnp.max(lg, axis=1, keepdims=True))
        pn_ref[hh] = (p / jnp.sum(p, axis=1, keepdims=True)).astype(BF16)
    acc = [jnp.zeros((16, HEAD_DIM), F32) for _ in range(A_HEADS)]
    for j in range(n_pg + 1):
        page = v_pages[j] if j < n_pg else v_new
        for hh in range(A_HEADS):
            acc[hh] = acc[hh] + _dot(pn_ref[hh, :, j * LANES:(j + 1) * LANES], page[hh].astype(BF16))
    o_ref[...] = jnp.concatenate(acc, axis=1)


def _dsa_sample(page_table, a, wcol, q16, idx_pool, ki_new, k_pool, k_new, v_pool, v_new, n_tok):
    nb = a.shape[0]
    topk = min(TOPK_MAX, (PAST_LEN + n_tok) // 4)
    n_keys = (N_PAGES + 1) * LANES

    def batch_spec(rows, width):
        return pl.BlockSpec((None, rows, width), lambda b, pt: (b, 0, 0))

    idx_page = lambda j: pl.BlockSpec((None, IDX_DIM, PAGE_SIZE), lambda b, pt, j=j: (pt[b, j], 0, 0))
    keys = pl.pallas_call(
        functools.partial(_dsa_sample_scores_kernel, n_tok=n_tok),
        grid_spec=pltpu.PrefetchScalarGridSpec(
            num_scalar_prefetch=1, grid=(nb,),
            in_specs=[batch_spec(n_tok * IDX_HEADS, IDX_DIM), batch_spec(n_tok * IDX_HEADS, 1)]
            + [idx_page(j) for j in range(N_PAGES)] + [batch_spec(IDX_DIM, PAGE_SIZE)],
            out_specs=batch_spec(8, n_keys)),
        out_shape=jax.ShapeDtypeStruct((nb, 8, n_keys), I32),
        compiler_params=_params(1),
        name="dsa_sample_scores",
    )(page_table, a, wcol, *([idx_pool] * N_PAGES), ki_new)

    rows = nb * 8
    tr = min(rows, 256)
    thr, keys = pl.pallas_call(
        functools.partial(_threshold_kernel, topk=topk),
        grid=(rows // tr,),
        in_specs=[pl.BlockSpec((tr, n_keys), lambda i: (i, 0))],
        out_specs=[pl.BlockSpec((tr, LANES), lambda i: (i, 0)), pl.BlockSpec((tr, n_keys), lambda i: (i, 0))],
        out_shape=[jax.ShapeDtypeStruct((rows, LANES), I32), jax.ShapeDtypeStruct((rows, n_keys), I32)],
        compiler_params=_params(1),
        name="dsa_sample_threshold",
    )(keys.reshape(rows, n_keys))
    thr, keys = thr.reshape(nb, 8, LANES), keys.reshape(nb, 8, n_keys)

    kv_page = lambda j: pl.BlockSpec((None, A_HEADS, PAGE_SIZE, HEAD_DIM), lambda b, pt, j=j: (pt[b, j], 0, 0, 0))
    kv_new = pl.BlockSpec((None, A_HEADS, PAGE_SIZE, HEAD_DIM), lambda b, pt: (b, 0, 0, 0))
    return pl.pallas_call(
        _dsa_sample_attend_kernel,
        grid_spec=pltpu.PrefetchScalarGridSpec(
            num_scalar_prefetch=1, grid=(nb,),
            in_specs=[batch_spec(16, A_WIDTH), batch_spec(8, n_keys), batch_spec(8, LANES)]
            + [kv_page(j) for j in range(N_PAGES)] + [kv_new] + [kv_page(j) for j in range(N_PAGES)] + [kv_new],
            out_specs=batch_spec(16, A_WIDTH),
            scratch_shapes=[pltpu.VMEM((A_HEADS, 16, n_keys), F32), pltpu.VMEM((A_HEADS, 16, n_keys), BF16)]),
        out_shape=jax.ShapeDtypeStruct((nb, 16, A_WIDTH), F32),
        compiler_params=_params(1),
        name="dsa_sample_attend",
    )(page_table, q16, keys, thr, *([k_pool] * N_PAGES), k_new, *([v_pool] * N_PAGES), v_new)


def _ret_consts(chunk):
    lg = jnp.log1p(-jnp.exp2(-5.0 - jnp.arange(R_HEADS, dtype=F32)))
    n = jnp.arange(chunk, dtype=F32)
    diff = n[:, None] - n[None, :]
    dmat = jnp.where(diff >= 0, jnp.exp(lg[:, None, None] * jnp.maximum(diff, 0.0)[None]), 0.0)
    cdec = jnp.exp(lg[:, None] * (n[None, :] + 1.0))
    kdec = jnp.exp(lg[:, None] * (chunk - 1.0 - n[None, :]))
    sdec = jnp.exp(lg * chunk)
    return dmat, cdec, kdec, sdec


def _ret_out_store(o, g, o_ref, sl):
    on = o * lax.rsqrt(jnp.mean(o * o, axis=-1, keepdims=True) + RMS_EPS)
    o_ref[:, sl] = (on * (g * jax.nn.sigmoid(g))).astype(o_ref.dtype)


def _ret_prompt_kernel(q_ref, k_ref, v_ref, g_ref, dmat_ref, cdec_ref, kdec_ref, sdec_ref,
                       o_ref, sout_ref, s_ref):
    c = pl.program_id(0)

    @pl.when(c == 0)
    def _():
        s_ref[...] = jnp.zeros_like(s_ref)

    for hh in range(R_HEADS):
        sl = slice(hh * R_DK, (hh + 1) * R_DK)
        k = k_ref[:, sl]
        qb = q_ref[:, sl].astype(BF16)
        vb = v_ref[:, sl].astype(BF16)
        att = _nt_dot(qb, k.astype(BF16)) * dmat_ref[hh]
        intra = _dot(att.astype(BF16), vb)
        s0 = s_ref[hh]
        cross = _dot(qb, s0.astype(BF16)) * cdec_ref[hh]
        kd_t = (k * kdec_ref[hh]).T.astype(BF16)
        s_ref[hh] = sdec_ref[hh] * s0 + _dot(kd_t, vb)
        _ret_out_store(intra + cross, g_ref[:, sl], o_ref, sl)

    @pl.when(c == pl.num_programs(0) - 1)
    def _():
        sout_ref[...] = s_ref[...]


def _retention_prompt(qr, kr, vg):
    t = qr.shape[0]
    c = RET_CHUNK
    dmat, cdec, kdec, sdec = _ret_consts(c)
    rep = lambda a: jnp.broadcast_to(a[:, :, None], (R_HEADS, c, LANES))
    cdec_b, kdec_b = rep(cdec), rep(kdec)
    sdec_b = jnp.broadcast_to(sdec[:, None, None], (R_HEADS, R_DK, LANES))
    blk = lambda col: pl.BlockSpec((c, R_WIDTH), lambda i, col=col: (i, col))
    const = lambda: pl.BlockSpec((R_HEADS, c, LANES), lambda i: (0, 0, 0))
    return pl.pallas_call(
        _ret_prompt_kernel,
        grid=(t // c,),
        in_specs=[blk(0), blk(0), blk(0), blk(1), const(), const(), const(), const()],
        out_specs=[pl.BlockSpec((c, R_WIDTH), lambda i: (i, 0)),
                   pl.BlockSpec((R_HEADS, R_DK, LANES), lambda i: (0, 0, 0))],
        out_shape=[jax.ShapeDtypeStruct((t, R_WIDTH), BF16),
                   jax.ShapeDtypeStruct((R_HEADS, R_DK, 128), F32)],
        scratch_shapes=[pltpu.VMEM((R_HEADS, R_DK, 128), F32)],
        compiler_params=_params(1),
        name="retention_prompt",
    )(qr, kr, vg, vg, dmat, cdec_b, kdec_b, sdec_b)


def _ret_sample_kernel(q_ref, k_ref, v_ref, g_ref, s0_ref, dmat_ref, cdec_ref, kdec_ref, sdec_ref,
                       o_ref, sout_ref, kpad_ref, vpad_ref):
    @pl.when(pl.program_id(0) == 0)
    def _():
        kpad_ref[...] = jnp.zeros_like(kpad_ref)
        vpad_ref[...] = jnp.zeros_like(vpad_ref)

    kpad_ref[0:8, :] = k_ref[...]
    vpad_ref[0:8, :] = v_ref[...]
    for hh in range(R_HEADS):
        sl = slice(hh * R_DK, (hh + 1) * R_DK)
        kp = kpad_ref[:, sl]
        vb = vpad_ref[:, sl].astype(BF16)
        qb = q_ref[:, sl].astype(BF16)
        att = _nt_dot(qb, kp.astype(BF16)) * dmat_ref[hh]
        intra = _dot(att.astype(BF16), vb)
        s0 = s0_ref[hh]
        cross = _dot(qb, s0.astype(BF16)) * cdec_ref[hh]
        kd_t = (kp * kdec_ref[hh]).T.astype(BF16)
        sout_ref[hh] = sdec_ref[hh] * s0 + _dot(kd_t, vb)
        _ret_out_store(intra + cross, g_ref[:, sl], o_ref, sl)


def _retention_sample(qr, kr, vr, gr, state, n_tok):
    nb = qr.shape[0]
    dmat, cdec, kdec, sdec = _ret_consts(n_tok)
    dmat_p = jnp.zeros((R_HEADS, 8, LANES), F32).at[:, :n_tok, :n_tok].set(dmat)
    cdec_p = jnp.zeros((R_HEADS, 8, LANES), F32).at[:, :n_tok, :].set(
        jnp.broadcast_to(cdec[:, :, None], (R_HEADS, n_tok, LANES)))
    kdec_p = jnp.zeros((R_HEADS, LANES, LANES), F32).at[:, :n_tok, :].set(
        jnp.broadcast_to(kdec[:, :, None], (R_HEADS, n_tok, LANES)))
    sdec_b = jnp.broadcast_to(sdec[:, None, None], (R_HEADS, R_DK, LANES))
    tok = lambda: pl.BlockSpec((None, 8, R_WIDTH), lambda b: (b, 0, 0))
    st = lambda: pl.BlockSpec((None, R_HEADS, R_DK, 128), lambda b: (b, 0, 0, 0))
    const = lambda rows: pl.BlockSpec((R_HEADS, rows, LANES), lambda b: (0, 0, 0))
    return pl.pallas_call(
        _ret_sample_kernel,
        grid=(nb,),
        in_specs=[tok(), tok(), tok(), tok(), st(), const(8), const(8), const(LANES), const(R_DK)],
        out_specs=[tok(), st()],
        out_shape=[jax.ShapeDtypeStruct((nb, 8, R_WIDTH), F32),
                   jax.ShapeDtypeStruct(state.shape, F32)],
        scratch_shapes=[pltpu.VMEM((LANES, R_WIDTH), F32), pltpu.VMEM((LANES, R_WIDTH), F32)],
        compiler_params=_params(1),
        name="retention_sample",
    )(qr, kr, vr, gr, state, dmat_p, cdec_p, kdec_p, sdec_b)


def _mem_attn_prompt_kernel(q_ref, k_ref, v_ref, o_ref):
    for hh in range(M_HEADS):
        sl = slice(hh * HEAD_DIM, (hh + 1) * HEAD_DIM)
        s = _nt_dot(q_ref[:, sl], k_ref[:, sl])
        p = jnp.exp2(s - jnp.max(s, axis=1, keepdims=True))
        pn = (p / jnp.sum(p, axis=1, keepdims=True)).astype(BF16)
        o_ref[:, sl] = _dot(pn, v_ref[:, sl]).astype(o_ref.dtype)


def _mem_attn_prompt(qm, mk, mv, tm):
    t = qm.shape[0]
    nm = mk.shape[0]
    return pl.pallas_call(
        _mem_attn_prompt_kernel,
        grid=(t // tm,),
        in_specs=[pl.BlockSpec((tm, M_WIDTH), lambda i: (i, 0)),
                  pl.BlockSpec((nm, M_WIDTH), lambda i: (0, 0)),
                  pl.BlockSpec((nm, M_WIDTH), lambda i: (0, 0))],
        out_specs=pl.BlockSpec((tm, M_WIDTH), lambda i: (i, 0)),
        out_shape=jax.ShapeDtypeStruct((t, M_WIDTH), BF16),
        compiler_params=_params(1),
        name="mem_attn_prompt",
    )(qm, mk, mv)


def _mem_attn_sample_kernel(q_ref, k_ref, v_ref, o_ref):
    nm = k_ref.shape[1] // M_HEADS
    for g in range(q_ref.shape[0]):
        outs = []
        for hh in range(M_HEADS):
            rows = pl.ds(hh, nm, stride=M_HEADS)
            s = _nt_dot(q_ref[g, :, hh * HEAD_DIM:(hh + 1) * HEAD_DIM], k_ref[g, rows, :].astype(BF16))
            p = jnp.exp2(s - jnp.max(s, axis=1, keepdims=True))
            pn = (p / jnp.sum(p, axis=1, keepdims=True)).astype(BF16)
            outs.append(_dot(pn, v_ref[g, rows, :].astype(BF16)))
        o_ref[g] = jnp.concatenate(outs, axis=1)


def _mem_attn_sample(q16, mem_k, mem_v):
    nb, nm = mem_k.shape[0], mem_k.shape[1]
    gb = 4 if nb % 4 == 0 else 1
    return pl.pallas_call(
        _mem_attn_sample_kernel,
        grid=(nb // gb,),
        in_specs=[pl.BlockSpec((gb, 16, M_WIDTH), lambda b: (b, 0, 0)),
                  pl.BlockSpec((gb, nm, HEAD_DIM), lambda b: (b, 0, 0)),
                  pl.BlockSpec((gb, nm, HEAD_DIM), lambda b: (b, 0, 0))],
        out_specs=pl.BlockSpec((gb, 16, M_WIDTH), lambda b: (b, 0, 0)),
        out_shape=jax.ShapeDtypeStruct((nb, 16, M_WIDTH), F32),
        compiler_params=_params(1),
        name="mem_attn_sample",
    )(q16, mem_k, mem_v)


def _merged_kernel(h_ref, oa_ref, ob_ref, om_ref, wg0_ref, wg1_ref, wg2_ref, wa_ref, wb_ref, wm_ref, o_ref):
    h = h_ref[...]
    acc = None
    for wg, o, wbr in ((wg0_ref, oa_ref, wa_ref), (wg1_ref, ob_ref, wb_ref), (wg2_ref, om_ref, wm_ref)):
        term = jax.nn.sigmoid(_dot(h, wg[...])) * _dot(o[...].astype(BF16), wbr[...])
        acc = term if acc is None else acc + term
    o_ref[...] = acc.astype(o_ref.dtype)


def _merged(h, o_a, o_b, o_m, w_gates, w_a, w_b, w_m, tm, tn):
    m = h.shape[0]
    nj = D_MODEL // tn
    row = lambda width: pl.BlockSpec((tm, width), lambda j, i: (i, 0))
    gate = lambda b: pl.BlockSpec((D_MODEL, tn), lambda j, i, b=b: (0, b * nj + j))
    br = lambda width: pl.BlockSpec((width, tn), lambda j, i: (0, j))
    return pl.pallas_call(
        _merged_kernel,
        grid=(nj, m // tm),
        in_specs=[row(D_MODEL), row(A_WIDTH), row(R_WIDTH), row(M_WIDTH), gate(0), gate(1), gate(2),
                  br(A_WIDTH), br(R_WIDTH), br(M_WIDTH)],
        out_specs=pl.BlockSpec((tm, tn), lambda j, i: (i, j)),
        out_shape=jax.ShapeDtypeStruct((m, D_MODEL), BF16),
        compiler_params=_params(2),
        name="merged",
    )(h, o_a, o_b, o_m, w_gates, w_gates, w_gates, w_a, w_b, w_m)


def _outproj_kernel(x_ref, mg_ref, wo_ref, n2_ref, wrt_hi_ref, wrt_lo_ref, x2_ref, h2_ref, rl_ref):
    x2 = x_ref[...] + _dot(mg_ref[...], wo_ref[...])
    x2_ref[...] = x2
    h2 = x2 * lax.rsqrt(jnp.mean(x2 * x2, axis=-1, keepdims=True) + RMS_EPS) * n2_ref[...]
    h2_hi = h2.astype(BF16)
    h2_ref[...] = h2_hi
    h2_lo = (h2 - h2_hi.astype(F32)).astype(BF16)
    rl_ref[...] = (_dot(h2_hi, wrt_hi_ref[...]) + _dot(h2_hi, wrt_lo_ref[...])) + _dot(h2_lo, wrt_hi_ref[...])


def _outproj(x, merged, w_out, norm2, w_rt, tm):
    m = x.shape[0]
    one = pl.Buffered(1)
    row = lambda width: pl.BlockSpec((tm, width), lambda i: (i, 0))
    w_rt_hi = w_rt.astype(BF16)
    w_rt_lo = (w_rt - w_rt_hi.astype(F32)).astype(BF16)
    return pl.pallas_call(
        _outproj_kernel,
        grid=(m // tm,),
        in_specs=[row(D_MODEL), row(D_MODEL),
                  pl.BlockSpec((D_MODEL, D_MODEL), lambda i: (0, 0), pipeline_mode=one),
                  pl.BlockSpec((1, D_MODEL), lambda i: (0, 0)),
                  pl.BlockSpec((D_MODEL, LANES), lambda i: (0, 0), pipeline_mode=one),
                  pl.BlockSpec((D_MODEL, LANES), lambda i: (0, 0), pipeline_mode=one)],
        out_specs=[row(D_MODEL), row(D_MODEL), row(LANES)],
        out_shape=[jax.ShapeDtypeStruct((m, D_MODEL), F32), jax.ShapeDtypeStruct((m, D_MODEL), BF16),
                   jax.ShapeDtypeStruct((m, LANES), F32)],
        compiler_params=_params(1),
        name="outproj",
    )(x, merged, w_out, norm2.reshape(1, D_MODEL), w_rt_hi, w_rt_lo)


def _router_kernel(rl_ref, comb_ref):
    z = rl_ref[...]
    lane = lax.broadcasted_iota(I32, z.shape, 1).astype(F32)
    ninf = -jnp.inf
    big = 1e9
    is_g = lane < N_GROUPS
    gl = jnp.where(is_g, z, ninf)
    gmax = jnp.max(gl, axis=1, keepdims=True)
    gsum = jnp.sum(jnp.where(is_g, jnp.exp(gl - gmax), 0.0), axis=1, keepdims=True)
    p_sel = 1.0 / gsum
    g_sel = jnp.min(jnp.where(gl == gmax, lane, big), axis=1, keepdims=True)
    lo = N_GROUPS + EXP_PER_GROUP * g_sel
    el = jnp.where(lane >= lo, jnp.where(lane < lo + EXP_PER_GROUP, z, ninf), ninf)
    v1 = jnp.max(el, axis=1, keepdims=True)
    i1 = jnp.min(jnp.where(el == v1, lane, big), axis=1, keepdims=True)
    el2 = jnp.where(lane == i1, ninf, el)
    v2 = jnp.max(el2, axis=1, keepdims=True)
    i2 = jnp.min(jnp.where(el2 == v2, lane, big), axis=1, keepdims=True)
    e2 = jnp.exp(v2 - v1)
    w1 = 1.0 / (1.0 + e2)
    w2 = e2 / (1.0 + e2)
    comb = jnp.where(lane == i1, w1 * p_sel, 0.0) + jnp.where(lane == i2, w2 * p_sel, 0.0)
    comb_ref[...] = jnp.where(lane == 0.0, g_sel, comb)


def _router(rl, tm):
    m = rl.shape[0]
    return pl.pallas_call(
        _router_kernel,
        grid=(m // tm,),
        in_specs=[pl.BlockSpec((tm, LANES), lambda i: (i, 0))],
        out_specs=pl.BlockSpec((tm, LANES), lambda i: (i, 0)),
        out_shape=jax.ShapeDtypeStruct((m, LANES), F32),
        compiler_params=_params(1),
        name="router",
    )(rl)


def _moe_kernel(x2_ref, h2_ref, comb_ref, wg_ref, wu_ref, wd_ref, y_ref, acc_ref, *, ne):
    k = pl.program_id(1)

    @pl.when(k == 0)
    def _():
        acc_ref[...] = x2_ref[...]

    h2 = h2_ref[...]
    comb = comb_ref[...]
    lane = lax.broadcasted_iota(I32, comb.shape, 1)
    for e in range(ne):
        ce = jnp.sum(jnp.where(lane == N_GROUPS + k * ne + e, comb, 0.0), axis=1, keepdims=True)
        a = _dot(h2, wg_ref[e])
        u = _dot(h2, wu_ref[e])
        act = (a * jax.nn.sigmoid(a)) * u * ce
        acc_ref[...] += _dot(act.astype(BF16), wd_ref[e])

    @pl.when(k == pl.num_programs(1) - 1)
    def _():
        y_ref[...] = acc_ref[...]


def _moe(x2, h2, comb, wg, wu, wd, tm, ne):
    m = x2.shape[0]
    row = lambda width: pl.BlockSpec((tm, width), lambda i, k: (i, 0))
    return pl.pallas_call(
        functools.partial(_moe_kernel, ne=ne),
        grid=(m // tm, N_EXPERTS // ne),
        in_specs=[row(D_MODEL), row(D_MODEL), row(LANES),
                  pl.BlockSpec((ne, D_MODEL, EXP_FF), lambda i, k: (k, 0, 0)),
                  pl.BlockSpec((ne, D_MODEL, EXP_FF), lambda i, k: (k, 0, 0)),
                  pl.BlockSpec((ne, EXP_FF, D_MODEL), lambda i, k: (k, 0, 0))],
        out_specs=row(D_MODEL),
        out_shape=jax.ShapeDtypeStruct((m, D_MODEL), F32),
        scratch_shapes=[pltpu.VMEM((tm, D_MODEL), F32)],
        compiler_params=_params(2),
        name="moe",
    )(x2, h2, comb, wg, wu, wd)


def _permute_rows_kernel(idx_ref, *refs, n_arrays, n_init, rows, scatter):
    srcs = refs[:n_arrays]
    dsts = refs[n_arrays + n_init:2 * n_arrays + n_init]
    sems = refs[2 * n_arrays + n_init]
    i = pl.program_id(0)

    def row_copies(src_row, dst_row):
        return [pltpu.make_async_copy(srcs[a].at[pl.ds(src_row, 1), :], dsts[a].at[pl.ds(dst_row, 1), :], sems.at[a])
                for a in range(n_arrays)]

    def start(r, carry):
        n = i * rows + r
        j = idx_ref[n]
        for cp in (row_copies(n, j) if scatter else row_copies(j, n)):
            cp.start()
        return carry

    def wait(r, carry):
        for cp in row_copies(r, r):
            cp.wait()
        return carry

    lax.fori_loop(0, rows, start, 0)

    @pl.when(i > 0)
    def _():
        lax.fori_loop(0, rows, wait, 0)

    @pl.when(i == pl.num_programs(0) - 1)
    def _():
        lax.fori_loop(0, rows, wait, 0)


def _permute_rows(idx, srcs, out_rows, scatter, name):
    n = idx.shape[0]
    rows = 256
    assert n % rows == 0
    inits = [jnp.zeros((out_rows, s.shape[1]), s.dtype) for s in srcs] if scatter else []
    n_arrays, n_init = len(srcs), len(inits)
    any_spec = pl.BlockSpec(memory_space=pl.ANY)
    out_shape = [jax.ShapeDtypeStruct((out_rows, s.shape[1]), s.dtype) for s in srcs]
    return pl.pallas_call(
        functools.partial(_permute_rows_kernel, n_arrays=n_arrays, n_init=n_init, rows=rows, scatter=scatter),
        grid_spec=pltpu.PrefetchScalarGridSpec(
            num_scalar_prefetch=1, grid=(n // rows,),
            in_specs=[any_spec] * (n_arrays + n_init), out_specs=[any_spec] * n_arrays,
            scratch_shapes=[pltpu.SemaphoreType.DMA((n_arrays,))]),
        out_shape=out_shape,
        input_output_aliases={1 + n_arrays + a: a for a in range(n_init)},
        compiler_params=_params(1),
        name=name,
    )(idx, *srcs, *inits)


def _moe_grouped_kernel(tg_ref, xs_ref, cs_ref, n2_ref, wg_ref, wu_ref, wd_ref, ys_ref, acc_ref, hn_ref, *, ne):
    t, k = pl.program_id(0), pl.program_id(1)

    @pl.when(k == 0)
    def _():
        x = xs_ref[...]
        acc_ref[...] = x
        hn = x * lax.rsqrt(jnp.mean(x * x, axis=-1, keepdims=True) + RMS_EPS) * n2_ref[...]
        hn_ref[...] = hn.astype(BF16)

    h2 = hn_ref[...]
    comb = cs_ref[...]
    lane = lax.broadcasted_iota(I32, comb.shape, 1)
    first = N_GROUPS + tg_ref[t] * EXP_PER_GROUP + k * ne
    for e in range(ne):
        ce = jnp.sum(jnp.where(lane == first + e, comb, 0.0), axis=1, keepdims=True)
        a = _dot(h2, wg_ref[e])
        u = _dot(h2, wu_ref[e])
        act = (a * jax.nn.sigmoid(a)) * u * ce
        acc_ref[...] += _dot(act.astype(BF16), wd_ref[e])

    @pl.when(k == pl.num_programs(1) - 1)
    def _():
        ys_ref[...] = acc_ref[...]


def _moe_grouped(x2, comb, norm2, wg, wu, wd):
    n = x2.shape[0]
    tile, ne = 512, 2
    steps = EXP_PER_GROUP // ne
    n_tiles = -(-(n + N_GROUPS * (tile - 1)) // tile)
    g = comb[:, 0].astype(I32)
    onehot = (g[:, None] == jnp.arange(N_GROUPS, dtype=I32)[None, :]).astype(I32)
    rank = jnp.sum((jnp.cumsum(onehot, axis=0) - onehot) * onehot, axis=1)
    padded = (jnp.sum(onehot, axis=0) + tile - 1) // tile * tile
    ends = jnp.cumsum(padded)
    dest = (jnp.sum(onehot * (ends - padded)[None, :], axis=1) + rank).astype(I32)
    tile_group = jnp.minimum(jnp.sum((jnp.arange(n_tiles, dtype=I32) * tile)[:, None] >= ends[None, :], axis=1),
                             N_GROUPS - 1).astype(I32)

    xs, cs = _permute_rows(dest, [x2, comb], n_tiles * tile, True, "moe_dispatch")
    row = lambda width: pl.BlockSpec((tile, width), lambda t, k, tg: (t, 0))
    wspec = lambda shape: pl.BlockSpec((ne,) + shape, lambda t, k, tg: (tg[t] * steps + k, 0, 0))
    ys = pl.pallas_call(
        functools.partial(_moe_grouped_kernel, ne=ne),
        grid_spec=pltpu.PrefetchScalarGridSpec(
            num_scalar_prefetch=1, grid=(n_tiles, steps),
            in_specs=[row(D_MODEL), row(LANES), pl.BlockSpec((1, D_MODEL), lambda t, k, tg: (0, 0)),
                      wspec((D_MODEL, EXP_FF)), wspec((D_MODEL, EXP_FF)), wspec((EXP_FF, D_MODEL))],
            out_specs=row(D_MODEL),
            scratch_shapes=[pltpu.VMEM((tile, D_MODEL), F32), pltpu.VMEM((tile, D_MODEL), BF16)]),
        out_shape=jax.ShapeDtypeStruct((n_tiles * tile, D_MODEL), F32),
        compiler_params=_params(2),
        name="moe_grouped",
    )(tile_group, xs, cs, norm2.reshape(1, D_MODEL), wg, wu, wd)
    (y,) = _permute_rows(dest, [ys], n, False, "moe_collect")
    return y


def _rope_tables(pos):
    half = R_DK // 2
    freqs = ROPE_BASE ** (-jnp.arange(half, dtype=F32) / half)
    ang = pos.astype(F32)[:, None] * freqs[None, :]
    cos, sin = jnp.cos(ang), jnp.sin(ang)
    return jnp.concatenate([cos, cos], axis=1), jnp.concatenate([-sin, sin], axis=1)


def _layer_weights(l, w_in, w_mem_kv, w_br_a, w_br_b, w_br_m, w_out, w_group, w_router,
                   w_gate_e, w_up_e, w_down_e):
    wi = w_in[l]
    col = lambda a, b: wi[:, a:b].astype(BF16)
    w_kw = jnp.pad(wi[:, O_KI:O_QR], ((0, 0), (0, LANES - (O_QR - O_KI)))).astype(BF16)
    w_rt = jnp.pad(jnp.concatenate([w_group[l], w_router[l]], axis=1),
                   ((0, 0), (0, LANES - N_GROUPS - N_EXPERTS)))
    return dict(
        qa=col(O_QA, O_KA), ka=col(O_KA, O_VA), va=col(O_VA, O_QI), qi=col(O_QI, O_KI), kw=w_kw,
        qr=col(O_QR, O_KR), kr=col(O_KR, O_VR), vg=col(O_VR, O_QM), qm=col(O_QM, O_GATES),
        gates=col(O_GATES, O_END),
        mk=w_mem_kv[l][:, :M_WIDTH].astype(BF16), mv=w_mem_kv[l][:, M_WIDTH:].astype(BF16),
        br_a=w_br_a[l].astype(BF16), br_b=w_br_b[l].astype(BF16), br_m=w_br_m[l].astype(BF16),
        out=w_out[l].astype(BF16), rt=w_rt,
        ge=w_gate_e[l].astype(BF16), ue=w_up_e[l].astype(BF16), de=w_down_e[l].astype(BF16))


def _project(x2d, pos, w, norms, tm):
    t = x2d.shape[0]
    norm1, q_norm_a, k_norm_a, idx_k_norm, q_norm_m = norms
    h = _rmsnorm_cast(x2d, norm1, tm)
    hd = lambda g: g.reshape(1, HEAD_DIM)
    gspec = _const_spec((1, HEAD_DIM))
    scale = HEAD_DIM ** -0.5 * LOG2_E
    (qa,) = _proj(functools.partial(_proj_headnorm_kernel, scale=scale), h, w["qa"], [hd(q_norm_a)], [gspec],
                  [(BF16, ROW)], tm, A_WIDTH, "proj_qa")
    ka, kab = _proj(functools.partial(_proj_headnorm_kernel, scale=1.0), h, w["ka"], [hd(k_norm_a)], [gspec],
                    [(F32, HEAD_MAJOR), (BF16, HEAD_MAJOR)], tm, A_WIDTH, "proj_ka")
    va, vab = _proj(_proj_plain_kernel, h, w["va"], [], [], [(F32, HEAD_MAJOR), (BF16, HEAD_MAJOR)], tm, A_WIDTH,
                    "proj_va")
    (qi,) = _proj(_proj_plain_kernel, h, w["qi"], [], [], [(BF16, ROW)], tm, 512, "proj_qi")
    g_idx = jnp.pad(idx_k_norm, (0, LANES - IDX_DIM)).reshape(1, LANES)
    kw, kt, kte, kto = _proj(_proj_kw_kernel, h, w["kw"], [g_idx], [_const_spec((1, LANES))],
                             [(F32, ROW), (F32, TRANSPOSED), (BF16, TRANSPOSED), (BF16, TRANSPOSED)], tm, LANES,
                             "proj_kw")
    cos2, sin2 = _rope_tables(pos)
    tabs, tab_specs = [cos2, sin2], [_row_spec(tm, R_DK), _row_spec(tm, R_DK)]
    (qr,) = _proj(functools.partial(_proj_rope_kernel, scale=1.0), h, w["qr"], tabs, tab_specs,
                  [(F32, ROW)], tm, R_WIDTH, "proj_qr")
    (kr,) = _proj(functools.partial(_proj_rope_kernel, scale=R_DK ** -0.5), h, w["kr"], tabs, tab_specs,
                  [(F32, ROW)], tm, R_WIDTH, "proj_kr")
    (vg,) = _proj(_proj_plain_kernel, h, w["vg"], [], [], [(F32, ROW)], tm, R_WIDTH, "proj_vg")
    (qm,) = _proj(functools.partial(_proj_headnorm_kernel, scale=scale), h, w["qm"], [hd(q_norm_m)], [gspec],
                  [(BF16, ROW)], tm, M_WIDTH, "proj_qm")
    return dict(h=h, qa=qa, ka=ka, kab=kab, va=va, vab=vab, qi=qi, kw=kw, kt=kt, kte=kte, kto=kto,
                qr=qr, kr=kr, vg=vg, qm=qm)


def _finish(x2d, h, o_a, o_b, o_m, w, norm2, tm):
    merged = _merged(h, o_a, o_b, o_m, w["gates"], w["br_a"], w["br_b"], w["br_m"], tm, 512)
    x2, h2, rl = _outproj(x2d, merged, w["out"], norm2, w["rt"], min(tm, 256))
    comb = _router(rl, tm)
    if x2.shape[0] >= 4 * 512:
        return _moe_grouped(x2, comb, norm2, w["ge"], w["ue"], w["de"])
    return _moe(x2, h2, comb, w["ge"], w["ue"], w["de"], tm, 2)


def kernel(x_prompt, x_sample, mem_prompt, cache_k, cache_v, cache_idx_k, state_ret, cache_mem_k, cache_mem_v,
           page_table, norm1, w_in, q_norm_a, k_norm_a, idx_k_norm, q_norm_m, k_norm_m, mem_norm, w_mem_kv,
           w_br_a, w_br_b, w_br_m, w_out, norm2, w_group, w_router, w_gate_e, w_up_e, w_down_e):
    depth = w_in.shape[0]
    bp, seq, _ = x_prompt.shape
    nb, n_tok, _ = x_sample.shape
    assert bp == 1 and n_tok <= 8
    y_p = x_prompt.reshape(seq, D_MODEL)
    y_s = x_sample.reshape(nb * n_tok, D_MODEL)
    pos_p = jnp.arange(seq)
    pos_s = jnp.tile(PAST_LEN + jnp.arange(n_tok), nb)
    outs = [[] for _ in range(10)]
    for l in range(depth):
        w = _layer_weights(l, w_in, w_mem_kv, w_br_a, w_br_b, w_br_m, w_out, w_group, w_router,
                           w_gate_e, w_up_e, w_down_e)
        norms = (norm1[l], q_norm_a[l], k_norm_a[l], idx_k_norm[l], q_norm_m[l])

        tm = 512
        p = _project(y_p, pos_p, w, norms, tm)
        o_a = _dsa_prompt(p["qa"], p["qi"], p["kw"], p["kte"], p["kto"], p["kab"], p["vab"])
        o_r, s_fin = _retention_prompt(p["qr"], p["kr"], p["vg"])
        n_mem = mem_prompt.shape[1]
        hm = _rmsnorm_cast(mem_prompt.reshape(n_mem, D_MODEL), mem_norm[l], n_mem)
        mk, mkb = _proj(functools.partial(_proj_headnorm_kernel, scale=1.0), hm, w["mk"],
                        [k_norm_m[l].reshape(1, HEAD_DIM)], [_const_spec((1, HEAD_DIM))],
                        [(F32, ROW), (BF16, ROW)], n_mem, M_WIDTH, "proj_mk")
        mv, mvb = _proj(_proj_plain_kernel, hm, w["mv"], [], [], [(F32, ROW), (BF16, ROW)],
                        n_mem, M_WIDTH, "proj_mv")
        o_m = _mem_attn_prompt(p["qm"], mkb, mvb, tm)
        y_p = _finish(y_p, p["h"], o_a, o_r, o_m, w, norm2[l], tm)
        outs[0].append(jnp.transpose(p["ka"], (1, 0, 2)).reshape(bp, seq, A_HEADS, HEAD_DIM))
        outs[1].append(jnp.transpose(p["va"], (1, 0, 2)).reshape(bp, seq, A_HEADS, HEAD_DIM))
        outs[2].append(p["kt"][:IDX_DIM].T.reshape(bp, seq, IDX_DIM))
        outs[3].append(s_fin.reshape(bp, R_HEADS, R_DK, 128))
        outs[4].append(mk.reshape(bp, n_mem, M_HEADS, HEAD_DIM))
        outs[5].append(mv.reshape(bp, n_mem, M_HEADS, HEAD_DIM))

        ts = nb * n_tok
        s = _project(y_s, pos_s, w, norms, ts)
        pad_tok = lambda a: jnp.pad(a.reshape(nb, n_tok, -1), ((0, 0), (0, 8 - n_tok), (0, 0)))
        a_idx = s["qi"].reshape(nb, n_tok * IDX_HEADS, IDX_DIM)
        w_idx = s["kw"][:, IDX_DIM:IDX_DIM + IDX_HEADS].reshape(nb, n_tok * IDX_HEADS, 1)
        pad16 = lambda a: jnp.pad(a.reshape(nb, n_tok, -1), ((0, 0), (0, 16 - n_tok), (0, 0)))
        ki_new = jnp.pad(jnp.transpose(s["kte"][:IDX_DIM].reshape(IDX_DIM, nb, n_tok), (1, 0, 2)),
                         ((0, 0), (0, 0), (0, PAGE_SIZE - n_tok)))
        new_page = lambda a: jnp.pad(jnp.transpose(a.reshape(A_HEADS, nb, n_tok, HEAD_DIM), (1, 0, 2, 3)),
                                     ((0, 0), (0, 0), (0, PAGE_SIZE - n_tok), (0, 0)))
        head_major_pool = lambda c: jnp.transpose(c, (0, 2, 1, 3))
        o_a_s = _dsa_sample(page_table, a_idx, w_idx, pad16(s["qa"]),
                            jnp.swapaxes(cache_idx_k[l], 1, 2), ki_new,
                            head_major_pool(cache_k[l]), new_page(s["kab"]),
                            head_major_pool(cache_v[l]), new_page(s["vab"]), n_tok)
        o_r_s, s_new = _retention_sample(pad_tok(s["qr"]), pad_tok(s["kr"]), pad_tok(s["vg"][:, :R_WIDTH]),
                                         pad_tok(s["vg"][:, R_WIDTH:]), state_ret[l], n_tok)
        mem_rows = lambda c: c.reshape(nb, -1, HEAD_DIM)
        o_m_s = _mem_attn_sample(pad16(s["qm"]), mem_rows(cache_mem_k[l]), mem_rows(cache_mem_v[l]))
        unpad = lambda a: a[:, :n_tok, :].reshape(ts, -1)
        y_s = _finish(y_s, s["h"], unpad(o_a_s), unpad(o_r_s), unpad(o_m_s), w, norm2[l], ts)
        outs[6].append(jnp.transpose(s["ka"], (1, 0, 2)).reshape(nb, n_tok, A_HEADS, HEAD_DIM))
        outs[7].append(jnp.transpose(s["va"], (1, 0, 2)).reshape(nb, n_tok, A_HEADS, HEAD_DIM))
        outs[8].append(s["kt"][:IDX_DIM].T.reshape(nb, n_tok, IDX_DIM))
        outs[9].append(s_new)
    stk = [jnp.stack(o) for o in outs]
    return (y_p.reshape(bp, seq, D_MODEL), y_s.reshape(nb, n_tok, D_MODEL), *stk)
```

```python
import functools

import jax
import jax.numpy as jnp
import numpy as np
from jax import lax
from jax.experimental import pallas as pl
from jax.experimental.pallas import tpu as pltpu

F32 = jnp.float32
BF16 = jnp.bfloat16
I32 = jnp.int32

D_MODEL = 2048
PAST_LEN = 2048
PAGE_SIZE = 128
N_PAGES = PAST_LEN // PAGE_SIZE
HEAD_DIM = 128
A_HEADS = 6
A_WIDTH = A_HEADS * HEAD_DIM
IDX_HEADS = 16
IDX_DIM = 64
TOPK_MAX = 256
R_HEADS = 6
R_DK = 128
R_WIDTH = R_HEADS * 128
RET_CHUNK = 128
ROPE_BASE = 10000.0
M_HEADS = 4
M_WIDTH = M_HEADS * HEAD_DIM
N_BRANCH = 3
N_GROUPS = 4
EXP_PER_GROUP = 8
N_EXPERTS = N_GROUPS * EXP_PER_GROUP
EXP_FF = 256
RMS_EPS = 1e-6

LANES = 128
INT_MIN = -(2 ** 31)
NEG_BIG = -1e30
LOG2_E = 1.4426950408889634
V7X_VMEM_LIMIT = 56 * 1024 * 1024

_SIZES = (A_WIDTH, A_WIDTH, A_WIDTH, IDX_HEADS * IDX_DIM, IDX_DIM, IDX_HEADS,
          R_HEADS * R_DK, R_HEADS * R_DK, R_WIDTH, R_WIDTH, M_WIDTH, N_BRANCH * D_MODEL)
_OFFS = tuple(int(v) for v in np.cumsum((0,) + _SIZES))
(O_QA, O_KA, O_VA, O_QI, O_KI, O_WI, O_QR, O_KR, O_VR, O_GR, O_QM, O_GATES, O_END) = _OFFS


def _params(n_grid, vmem=V7X_VMEM_LIMIT):
    return pltpu.CompilerParams(dimension_semantics=("arbitrary",) * n_grid, vmem_limit_bytes=vmem)


def _nt_dot(a, b):
    return lax.dot_general(a, b, (((1,), (1,)), ((), ())), preferred_element_type=F32)


def _dot(a, b):
    return jnp.dot(a, b, preferred_element_type=F32)


def _rmsnorm_kernel(x_ref, g_ref, o_ref):
    x = x_ref[...]
    ms = jnp.mean(x * x, axis=-1, keepdims=True)
    o_ref[...] = (x * lax.rsqrt(ms + RMS_EPS) * g_ref[...]).astype(o_ref.dtype)


def _rmsnorm_cast(x, g, tm):
    m, d = x.shape
    return pl.pallas_call(
        _rmsnorm_kernel,
        grid=(m // tm,),
        in_specs=[pl.BlockSpec((tm, d), lambda i: (i, 0)), pl.BlockSpec((1, d), lambda i: (0, 0))],
        out_specs=pl.BlockSpec((tm, d), lambda i: (i, 0)),
        out_shape=jax.ShapeDtypeStruct((m, d), BF16),
        compiler_params=_params(1),
        name="rmsnorm_cast",
    )(x, g.reshape(1, d))


def _store_cols(o_ref, hh, z):
    if len(o_ref.shape) == 3:
        o_ref[hh] = z.astype(o_ref.dtype)
    else:
        o_ref[:, hh * HEAD_DIM:(hh + 1) * HEAD_DIM] = z.astype(o_ref.dtype)


def _proj_plain_kernel(h_ref, w_ref, *o_refs):
    y = _dot(h_ref[...], w_ref[...])
    for o in o_refs:
        if len(o.shape) == 3:
            for hh in range(o.shape[0]):
                o[hh] = y[:, hh * HEAD_DIM:(hh + 1) * HEAD_DIM].astype(o.dtype)
        else:
            o[...] = y.astype(o.dtype)


def _proj_headnorm_kernel(h_ref, w_ref, g_ref, *o_refs, scale):
    y = _dot(h_ref[...], w_ref[...])
    g = g_ref[...]
    for hh in range(y.shape[1] // HEAD_DIM):
        sl = slice(hh * HEAD_DIM, (hh + 1) * HEAD_DIM)
        yh = y[:, sl]
        ms = jnp.mean(yh * yh, axis=-1, keepdims=True)
        z = yh * lax.rsqrt(ms + RMS_EPS) * g
        if scale != 1.0:
            z = z * scale
        for o in o_refs:
            _store_cols(o, hh, z)


def _proj_rope_kernel(h_ref, w_ref, cos_ref, sin_ref, o_ref, *, scale):
    y = _dot(h_ref[...], w_ref[...])
    c = cos_ref[...]
    s = sin_ref[...]
    for hh in range(y.shape[1] // R_DK):
        sl = slice(hh * R_DK, (hh + 1) * R_DK)
        yh = y[:, sl]
        z = yh * c + pltpu.roll(yh, R_DK // 2, 1) * s
        if scale != 1.0:
            z = z * scale
        o_ref[:, sl] = z


def _proj_kw_kernel(h_ref, w_ref, g_ref, kw_ref, kt_ref, kte_ref, kto_ref):
    y = _dot(h_ref[...], w_ref[...])
    lane = lax.broadcasted_iota(I32, y.shape, 1)
    is_k = lane < IDX_DIM
    ms = jnp.sum(jnp.where(is_k, y * y, 0.0), axis=-1, keepdims=True) * (1.0 / IDX_DIM)
    kn = y * lax.rsqrt(ms + RMS_EPS) * g_ref[...]
    kw_ref[...] = jnp.where(is_k, kn, y * (IDX_HEADS * IDX_DIM) ** -0.5)
    et = jnp.where(is_k, kn, 0.0).T
    kt_ref[...] = et
    kte_ref[...] = et.astype(BF16)
    kto_ref[...] = pltpu.roll(et, IDX_DIM, 0).astype(BF16)


ROW, TRANSPOSED, HEAD_MAJOR = "row", "transposed", "head_major"


def _proj(kern, h, w, extras, extra_specs, outs, tm, tn, name):
    m, k = h.shape
    n = w.shape[1]
    out_shape, out_specs = [], []
    for dt, layout in outs:
        if layout == TRANSPOSED:
            out_shape.append(jax.ShapeDtypeStruct((n, m), dt))
            out_specs.append(pl.BlockSpec((tn, tm), lambda j, i: (j, i)))
        elif layout == HEAD_MAJOR:
            out_shape.append(jax.ShapeDtypeStruct((n // HEAD_DIM, m, HEAD_DIM), dt))
            out_specs.append(pl.BlockSpec((tn // HEAD_DIM, tm, HEAD_DIM), lambda j, i: (j, i, 0)))
        else:
            out_shape.append(jax.ShapeDtypeStruct((m, n), dt))
            out_specs.append(pl.BlockSpec((tm, tn), lambda j, i: (i, j)))
    res = pl.pallas_call(
        kern,
        grid=(n // tn, m // tm),
        in_specs=[pl.BlockSpec((tm, k), lambda j, i: (i, 0)),
                  pl.BlockSpec((k, tn), lambda j, i: (0, j))] + list(extra_specs),
        out_specs=out_specs,
        out_shape=out_shape,
        compiler_params=_params(2),
        name=name,
    )(h, w, *extras)
    return res


def _row_spec(tm, width):
    return pl.BlockSpec((tm, width), lambda j, i: (i, 0))


def _const_spec(shape):
    return pl.BlockSpec(shape, lambda j, i: (0,) * len(shape))


def _score_key(score):
    bits = lax.bitcast_convert_type(score, I32)
    return jnp.where(bits < 0, bits ^ jnp.int32(0x7FFFFFFF), bits)


def _bisect_threshold(count_ge, rows, topk):
    def bit_body(it, carry):
        ua, cnt = carry
        cand_u = ua | jnp.left_shift(jnp.int32(1), 31 - it)
        total = count_ge(cand_u ^ INT_MIN)
        ok = total >= topk
        return jnp.where(ok, cand_u, ua), jnp.where(ok, jnp.broadcast_to(total, cnt.shape), cnt)

    init = (jnp.zeros((rows, LANES), I32), jnp.zeros((rows, LANES), F32))
    ua, cnt = lax.fori_loop(0, 32, bit_body, init)
    return jnp.maximum(ua ^ INT_MIN, INT_MIN + 1), cnt


def _demote_excess_ties(keys_ref, thr, need, n_pieces):
    rows = keys_ref.shape[0]
    tri = jnp.where(lax.broadcasted_iota(I32, (LANES, LANES), 0) <= lax.broadcasted_iota(I32, (LANES, LANES), 1),
                    1.0, 0.0).astype(BF16)

    def piece(j, seen):
        off = pl.multiple_of(j * LANES, LANES)
        k = keys_ref[:, pl.ds(off, LANES)]
        tie = jnp.where(k == thr, 1.0, 0.0)
        rank = seen + _dot(tie.astype(BF16), tri)
        keys_ref[:, pl.ds(off, LANES)] = jnp.where(tie * rank > need, INT_MIN, k)
        return jnp.broadcast_to(rank[:, LANES - 1:LANES], (rows, LANES))

    lax.fori_loop(0, n_pieces, piece, jnp.zeros((rows, LANES), F32))


def _dsa_prompt_kernel(qa_ref, qi_ref, kw_ref, kte_ref, kto_ref, k_ref, v_ref, o_ref,
                       keys_ref, wb_ref, m_ref, l_ref, acc_ref, a_ref, s_ref, p_ref, *, topk, tq, ts, tc, tk):
    i = pl.program_id(0)
    n_vis = i * tq + tq
    n_sc = (n_vis + ts - 1) // ts
    n_ch = (n_vis + tk - 1) // tk
    kw = kw_ref[...]
    for hh in range(IDX_HEADS):
        wb_ref[hh] = jnp.broadcast_to(kw[:, IDX_DIM + hh:IDX_DIM + hh + 1], (tq, LANES))
    row = i * tq + lax.broadcasted_iota(I32, (tq, LANES), 0)
    lane = lax.broadcasted_iota(I32, (tq, LANES), 1)

    def score_chunk(c, carry):
        off = pl.multiple_of(c * ts, ts)
        kte = kte_ref[:, pl.ds(off, ts)]
        kto = kto_ref[:, pl.ds(off, ts)]
        acc = [jnp.zeros((tq, LANES), F32) for _ in range(ts // LANES)]
        for p in range(IDX_HEADS // 2):
            lhs = qi_ref[:, p * LANES:(p + 1) * LANES]
            for w_head, rhs in ((wb_ref[2 * p], kte), (wb_ref[2 * p + 1], kto)):
                r = jnp.maximum(_dot(lhs, rhs), 0.0)
                for u in range(ts // LANES):
                    acc[u] = acc[u] + w_head * r[:, u * LANES:(u + 1) * LANES]
        for u in range(ts // LANES):
            col = off + u * LANES + lane
            keys_ref[:, pl.ds(off + u * LANES, LANES)] = jnp.where(col <= row, _score_key(acc[u]), INT_MIN)
        return carry

    lax.fori_loop(0, n_sc, score_chunk, 0)

    n_cc = (n_vis + tc - 1) // tc

    def pad_chunk(c, carry):
        keys_ref[:, pl.ds(pl.multiple_of(c * ts, ts), ts)] = jnp.full((tq, ts), INT_MIN, I32)
        return carry

    lax.fori_loop(n_sc, n_cc * (tc // ts), pad_chunk, 0)

    def count(pred):
        def count_chunk(c, cnt):
            off = pl.multiple_of(c * tc, tc)
            for u in range(tc // LANES):
                cnt = cnt + jnp.where(pred(keys_ref[:, pl.ds(off + u * LANES, LANES)]), 1.0, 0.0)
            return cnt

        cnt = lax.fori_loop(0, n_cc, count_chunk, jnp.zeros((tq, LANES), F32))
        return jnp.sum(cnt, axis=1, keepdims=True)

    thr, cnt_ge = _bisect_threshold(lambda cand: count(lambda k: k >= cand), tq, topk)

    @pl.when(jnp.max(cnt_ge) > topk)
    def _():
        need = topk - count(lambda k: k > thr)
        _demote_excess_ties(keys_ref, thr, need, n_cc * (tc // LANES))

    m_ref[...] = jnp.full(m_ref.shape, NEG_BIG, F32)
    l_ref[...] = jnp.zeros(l_ref.shape, F32)
    acc_ref[...] = jnp.zeros(acc_ref.shape, F32)

    def att_chunk(c, carry):
        off = pl.multiple_of(c * tk, tk)
        n_u = tk // LANES
        bias = jnp.concatenate(
            [jnp.where(keys_ref[:, pl.ds(off + u * LANES, LANES)] >= thr, 0.0, NEG_BIG) for u in range(n_u)], axis=1)
        half = tk // 2
        for hh in range(A_HEADS):
            q = qa_ref[:, hh * HEAD_DIM:(hh + 1) * HEAD_DIM]
            for part in range(2):
                s_ref[hh, :, part * half:(part + 1) * half] = (
                    _nt_dot(q, k_ref[hh, pl.ds(off + part * half, half), :]) + bias[:, part * half:(part + 1) * half])
        for hh in range(A_HEADS):
            m_old = m_ref[hh]
            mx = s_ref[hh, :, 0:LANES]
            for u in range(1, n_u):
                mx = jnp.maximum(mx, s_ref[hh, :, u * LANES:(u + 1) * LANES])
            m_new = jnp.maximum(m_old, jnp.broadcast_to(jnp.max(mx, axis=1, keepdims=True), (tq, LANES)))
            alpha = jnp.exp2(m_old - m_new)
            psum = jnp.zeros((tq, LANES), F32)
            for u in range(n_u):
                p = jnp.exp2(s_ref[hh, :, u * LANES:(u + 1) * LANES] - m_new)
                psum = psum + p
                p_ref[hh, :, u * LANES:(u + 1) * LANES] = p.astype(BF16)
            l_ref[hh] = alpha * l_ref[hh] + jnp.broadcast_to(jnp.sum(psum, axis=1, keepdims=True), (tq, LANES))
            m_ref[hh] = m_new
            a_ref[hh] = alpha
        for hh in range(A_HEADS):
            acc_ref[hh] = a_ref[hh] * acc_ref[hh] + _dot(p_ref[hh], v_ref[hh, pl.ds(off, tk), :])
        return carry

    lax.fori_loop(0, n_ch, att_chunk, 0)
    for hh in range(A_HEADS):
        o_ref[:, hh * HEAD_DIM:(hh + 1) * HEAD_DIM] = (acc_ref[hh] / l_ref[hh]).astype(o_ref.dtype)


def _dsa_prompt(qa, qi, kw, kte, kto, kb, vb):
    t = qa.shape[0]
    tq, ts, tk = 128, 256, 512
    tc = 1024 if t % 1024 == 0 else tk
    assert t % tk == 0 and tc % tk == 0 and tk % ts == 0
    topk = min(TOPK_MAX, t // 4)
    one = pl.Buffered(1)
    head_state = pltpu.VMEM((A_HEADS, tq, LANES), F32)
    return pl.pallas_call(
        functools.partial(_dsa_prompt_kernel, topk=topk, tq=tq, ts=ts, tc=tc, tk=tk),
        grid=(t // tq,),
        in_specs=[pl.BlockSpec((tq, A_WIDTH), lambda i: (i, 0)),
                  pl.BlockSpec((tq, IDX_HEADS * IDX_DIM), lambda i: (i, 0)),
                  pl.BlockSpec((tq, LANES), lambda i: (i, 0)),
                  pl.BlockSpec((LANES, t), lambda i: (0, 0), pipeline_mode=one),
                  pl.BlockSpec((LANES, t), lambda i: (0, 0), pipeline_mode=one),
                  pl.BlockSpec((A_HEADS, t, HEAD_DIM), lambda i: (0, 0, 0), pipeline_mode=one),
                  pl.BlockSpec((A_HEADS, t, HEAD_DIM), lambda i: (0, 0, 0), pipeline_mode=one)],
        out_specs=pl.BlockSpec((tq, A_WIDTH), lambda i: (i, 0)),
        out_shape=jax.ShapeDtypeStruct((t, A_WIDTH), BF16),
        scratch_shapes=[pltpu.VMEM((tq, t), I32), pltpu.VMEM((IDX_HEADS, tq, LANES), F32),
                        head_state, head_state, head_state, head_state,
                        pltpu.VMEM((A_HEADS, tq, tk), F32), pltpu.VMEM((A_HEADS, tq, tk), BF16)],
        compiler_params=_params(1),
        name="dsa_prompt",
    )(qa, qi, kw, kte, kto, kb, vb)


def _dsa_sample_scores_kernel(pt_ref, a_ref, w_ref, *refs, n_tok):
    idx_pages, ki_new, keys_ref = refs[:N_PAGES], refs[N_PAGES], refs[N_PAGES + 1]
    a = a_ref[...]
    wcol = jnp.broadcast_to(w_ref[...], (n_tok * IDX_HEADS, LANES))
    sub8 = lax.broadcasted_iota(I32, (8, LANES), 0)
    lane8 = lax.broadcasted_iota(I32, (8, LANES), 1)
    for j in range(N_PAGES + 1):
        kp = idx_pages[j][...].astype(BF16) if j < N_PAGES else ki_new[...]
        r = jnp.maximum(_dot(a, kp), 0.0) * wcol
        k8 = jnp.full((8, LANES), INT_MIN, I32)
        for t in range(n_tok):
            st = jnp.sum(r[t * IDX_HEADS:(t + 1) * IDX_HEADS, :], axis=0, keepdims=True)
            kt = jnp.broadcast_to(_score_key(st), (8, LANES))
            if j == N_PAGES:
                kt = jnp.where(lane8 <= t, kt, INT_MIN)
            k8 = jnp.where(sub8 == t, kt, k8)
        keys_ref[:, j * LANES:(j + 1) * LANES] = k8


def _threshold_kernel(keys_ref, thr_ref, kout_ref, *, topk):
    rows, n_keys = keys_ref.shape
    kout_ref[...] = keys_ref[...]

    def count(pred):
        cnt = jnp.zeros((rows, LANES), F32)
        for j in range(n_keys // LANES):
            cnt = cnt + jnp.where(pred(keys_ref[:, j * LANES:(j + 1) * LANES]), 1.0, 0.0)
        return jnp.sum(cnt, axis=1, keepdims=True)

    thr, cnt_ge = _bisect_threshold(lambda cand: count(lambda k: k >= cand), rows, topk)
    thr_ref[...] = thr

    @pl.when(jnp.max(cnt_ge) > topk)
    def _():
        need = topk - count(lambda k: k > thr)
        _demote_excess_ties(kout_ref, thr, need, n_keys // LANES)


def _dsa_sample_attend_kernel(pt_ref, q_ref, keys_ref, thr_ref, *refs):
    n_pg = N_PAGES
    k_pages, k_new = refs[:n_pg], refs[n_pg]
    v_pages, v_new = refs[n_pg + 1:2 * n_pg + 1], refs[2 * n_pg + 1]
    o_ref, lg_ref, pn_ref = refs[2 * n_pg + 2:]
    thr = thr_ref[...]
    for j in range(n_pg + 1):
        page = k_pages[j] if j < n_pg else k_new
        bias8 = jnp.where(keys_ref[:, j * LANES:(j + 1) * LANES] >= thr, 0.0, NEG_BIG)
        bias = jnp.concatenate([bias8, bias8], axis=0)
        for hh in range(A_HEADS):
            s = _nt_dot(q_ref[:, hh * HEAD_DIM:(hh + 1) * HEAD_DIM], page[hh].astype(BF16))
            lg_ref[hh, :, j * LANES:(j + 1) * LANES] = s + bias
    for hh in range(A_HEADS):
        lg = lg_ref[hh]
        p = jnp.exp2(lg - jnp.max(lg, axis=1, keepdims=True))
        pn_ref[hh] = (p / jnp.sum(p, axis=1, keepdims=True)).astype(BF16)
    acc = [jnp.zeros((16, HEAD_DIM), F32) for _ in range(A_HEADS)]
    for j in range(n_pg + 1):
        page = v_pages[j] if j < n_pg else v_new
        for hh in range(A_HEADS):
            acc[hh] = acc[hh] + _dot(pn_ref[hh, :, j * LANES:(j + 1) * LANES], page[hh].astype(BF16))
    o_ref[...] = jnp.concatenate(acc, axis=1)


def _dsa_sample(page_table, a, wcol, q16, idx_pool, ki_new, k_pool, k_new, v_pool, v_new, n_tok):
    nb = a.shape[0]
    topk = min(TOPK_MAX, (PAST_LEN + n_tok) // 4)
    n_keys = (N_PAGES + 1) * LANES

    def batch_spec(rows, width):
        return pl.BlockSpec((None, rows, width), lambda b, pt: (b, 0, 0))

    idx_page = lambda j: pl.BlockSpec((None, IDX_DIM, PAGE_SIZE), lambda b, pt, j=j: (pt[b, j], 0, 0))
    keys = pl.pallas_call(
        functools.partial(_dsa_sample_scores_kernel, n_tok=n_tok),
        grid_spec=pltpu.PrefetchScalarGridSpec(
            num_scalar_prefetch=1, grid=(nb,),
            in_specs=[batch_spec(n_tok * IDX_HEADS, IDX_DIM), batch_spec(n_tok * IDX_HEADS, 1)]
            + [idx_page(j) for j in range(N_PAGES)] + [batch_spec(IDX_DIM, PAGE_SIZE)],
            out_specs=batch_spec(8, n_keys)),
        out_shape=jax.ShapeDtypeStruct((nb, 8, n_keys), I32),
        compiler_params=_params(1),
        name="dsa_sample_scores",
    )(page_table, a, wcol, *([idx_pool] * N_PAGES), ki_new)

    rows = nb * 8
    tr = min(rows, 256)
    thr, keys = pl.pallas_call(
        functools.partial(_threshold_kernel, topk=topk),
        grid=(rows // tr,),
        in_specs=[pl.BlockSpec((tr, n_keys), lambda i: (i, 0))],
        out_specs=[pl.BlockSpec((tr, LANES), lambda i: (i, 0)), pl.BlockSpec((tr, n_keys), lambda i: (i, 0))],
        out_shape=[jax.ShapeDtypeStruct((rows, LANES), I32), jax.ShapeDtypeStruct((rows, n_keys), I32)],
        compiler_params=_params(1),
        name="dsa_sample_threshold",
    )(keys.reshape(rows, n_keys))
    thr, keys = thr.reshape(nb, 8, LANES), keys.reshape(nb, 8, n_keys)

    kv_page = lambda j: pl.BlockSpec((None, A_HEADS, PAGE_SIZE, HEAD_DIM), lambda b, pt, j=j: (pt[b, j], 0, 0, 0))
    kv_new = pl.BlockSpec((None, A_HEADS, PAGE_SIZE, HEAD_DIM), lambda b, pt: (b, 0, 0, 0))
    return pl.pallas_call(
        _dsa_sample_attend_kernel,
        grid_spec=pltpu.PrefetchScalarGridSpec(
            num_scalar_prefetch=1, grid=(nb,),
            in_specs=[batch_spec(16, A_WIDTH), batch_spec(8, n_keys), batch_spec(8, LANES)]
            + [kv_page(j) for j in range(N_PAGES)] + [kv_new] + [kv_page(j) for j in range(N_PAGES)] + [kv_new],
            out_specs=batch_spec(16, A_WIDTH),
            scratch_shapes=[pltpu.VMEM((A_HEADS, 16, n_keys), F32), pltpu.VMEM((A_HEADS, 16, n_keys), BF16)]),
        out_shape=jax.ShapeDtypeStruct((nb, 16, A_WIDTH), F32),
        compiler_params=_params(1),
        name="dsa_sample_attend",
    )(page_table, q16, keys, thr, *([k_pool] * N_PAGES), k_new, *([v_pool] * N_PAGES), v_new)


def _ret_consts(chunk):
    lg = jnp.log1p(-jnp.exp2(-5.0 - jnp.arange(R_HEADS, dtype=F32)))
    n = jnp.arange(chunk, dtype=F32)
    diff = n[:, None] - n[None, :]
    dmat = jnp.where(diff >= 0, jnp.exp(lg[:, None, None] * jnp.maximum(diff, 0.0)[None]), 0.0)
    cdec = jnp.exp(lg[:, None] * (n[None, :] + 1.0))
    kdec = jnp.exp(lg[:, None] * (chunk - 1.0 - n[None, :]))
    sdec = jnp.exp(lg * chunk)
    return dmat, cdec, kdec, sdec


def _ret_out_store(o, g, o_ref, sl):
    on = o * lax.rsqrt(jnp.mean(o * o, axis=-1, keepdims=True) + RMS_EPS)
    o_ref[:, sl] = (on * (g * jax.nn.sigmoid(g))).astype(o_ref.dtype)


def _ret_prompt_kernel(q_ref, k_ref, v_ref, g_ref, dmat_ref, cdec_ref, kdec_ref, sdec_ref,
                       o_ref, sout_ref, s_ref):
    c = pl.program_id(0)

    @pl.when(c == 0)
    def _():
        s_ref[...] = jnp.zeros_like(s_ref)

    for hh in range(R_HEADS):
        sl = slice(hh * R_DK, (hh + 1) * R_DK)
        k = k_ref[:, sl]
        qb = q_ref[:, sl].astype(BF16)
        vb = v_ref[:, sl].astype(BF16)
        att = _nt_dot(qb, k.astype(BF16)) * dmat_ref[hh]
        intra = _dot(att.astype(BF16), vb)
        s0 = s_ref[hh]
        cross = _dot(qb, s0.astype(BF16)) * cdec_ref[hh]
        kd_t = (k * kdec_ref[hh]).T.astype(BF16)
        s_ref[hh] = sdec_ref[hh] * s0 + _dot(kd_t, vb)
        _ret_out_store(intra + cross, g_ref[:, sl], o_ref, sl)

    @pl.when(c == pl.num_programs(0) - 1)
    def _():
        sout_ref[...] = s_ref[...]


def _retention_prompt(qr, kr, vg):
    t = qr.shape[0]
    c = RET_CHUNK
    dmat, cdec, kdec, sdec = _ret_consts(c)
    rep = lambda a: jnp.broadcast_to(a[:, :, None], (R_HEADS, c, LANES))
    cdec_b, kdec_b = rep(cdec), rep(kdec)
    sdec_b = jnp.broadcast_to(sdec[:, None, None], (R_HEADS, R_DK, LANES))
    blk = lambda col: pl.BlockSpec((c, R_WIDTH), lambda i, col=col: (i, col))
    const = lambda: pl.BlockSpec((R_HEADS, c, LANES), lambda i: (0, 0, 0))
    return pl.pallas_call(
        _ret_prompt_kernel,
        grid=(t // c,),
        in_specs=[blk(0), blk(0), blk(0), blk(1), const(), const(), const(), const()],
        out_specs=[pl.BlockSpec((c, R_WIDTH), lambda i: (i, 0)),
                   pl.BlockSpec((R_HEADS, R_DK, LANES), lambda i: (0, 0, 0))],
        out_shape=[jax.ShapeDtypeStruct((t, R_WIDTH), BF16),
                   jax.ShapeDtypeStruct((R_HEADS, R_DK, 128), F32)],
        scratch_shapes=[pltpu.VMEM((R_HEADS, R_DK, 128), F32)],
        compiler_params=_params(1),
        name="retention_prompt",
    )(qr, kr, vg, vg, dmat, cdec_b, kdec_b, sdec_b)


def _ret_sample_kernel(q_ref, k_ref, v_ref, g_ref, s0_ref, dmat_ref, cdec_ref, kdec_ref, sdec_ref,
                       o_ref, sout_ref, kpad_ref, vpad_ref):
    @pl.when(pl.program_id(0) == 0)
    def _():
        kpad_ref[...] = jnp.zeros_like(kpad_ref)
        vpad_ref[...] = jnp.zeros_like(vpad_ref)

    kpad_ref[0:8, :] = k_ref[...]
    vpad_ref[0:8, :] = v_ref[...]
    for hh in range(R_HEADS):
        sl = slice(hh * R_DK, (hh + 1) * R_DK)
        kp = kpad_ref[:, sl]
        vb = vpad_ref[:, sl].astype(BF16)
        qb = q_ref[:, sl].astype(BF16)
        att = _nt_dot(qb, kp.astype(BF16)) * dmat_ref[hh]
        intra = _dot(att.astype(BF16), vb)
        s0 = s0_ref[hh]
        cross = _dot(qb, s0.astype(BF16)) * cdec_ref[hh]
        kd_t = (kp * kdec_ref[hh]).T.astype(BF16)
        sout_ref[hh] = sdec_ref[hh] * s0 + _dot(kd_t, vb)
        _ret_out_store(intra + cross, g_ref[:, sl], o_ref, sl)


def _retention_sample(qr, kr, vr, gr, state, n_tok):
    nb = qr.shape[0]
    dmat, cdec, kdec, sdec = _ret_consts(n_tok)
    dmat_p = jnp.zeros((R_HEADS, 8, LANES), F32).at[:, :n_tok, :n_tok].set(dmat)
    cdec_p = jnp.zeros((R_HEADS, 8, LANES), F32).at[:, :n_tok, :].set(
        jnp.broadcast_to(cdec[:, :, None], (R_HEADS, n_tok, LANES)))
    kdec_p = jnp.zeros((R_HEADS, LANES, LANES), F32).at[:, :n_tok, :].set(
        jnp.broadcast_to(kdec[:, :, None], (R_HEADS, n_tok, LANES)))
    sdec_b = jnp.broadcast_to(sdec[:, None, None], (R_HEADS, R_DK, LANES))
    tok = lambda: pl.BlockSpec((None, 8, R_WIDTH), lambda b: (b, 0, 0))
    st = lambda: pl.BlockSpec((None, R_HEADS, R_DK, 128), lambda b: (b, 0, 0, 0))
    const = lambda rows: pl.BlockSpec((R_HEADS, rows, LANES), lambda b: (0, 0, 0))
    return pl.pallas_call(
        _ret_sample_kernel,
        grid=(nb,),
        in_specs=[tok(), tok(), tok(), tok(), st(), const(8), const(8), const(LANES), const(R_DK)],
        out_specs=[tok(), st()],
        out_shape=[jax.ShapeDtypeStruct((nb, 8, R_WIDTH), F32),
                   jax.ShapeDtypeStruct(state.shape, F32)],
        scratch_shapes=[pltpu.VMEM((LANES, R_WIDTH), F32), pltpu.VMEM((LANES, R_WIDTH), F32)],
        compiler_params=_params(1),
        name="retention_sample",
    )(qr, kr, vr, gr, state, dmat_p, cdec_p, kdec_p, sdec_b)


def _mem_attn_prompt_kernel(q_ref, k_ref, v_ref, o_ref):
    for hh in range(M_HEADS):
        sl = slice(hh * HEAD_DIM, (hh + 1) * HEAD_DIM)
        s = _nt_dot(q_ref[:, sl], k_ref[:, sl])
        p = jnp.exp2(s - jnp.max(s, axis=1, keepdims=True))
        pn = (p / jnp.sum(p, axis=1, keepdims=True)).astype(BF16)
        o_ref[:, sl] = _dot(pn, v_ref[:, sl]).astype(o_ref.dtype)


def _mem_attn_prompt(qm, mk, mv, tm):
    t = qm.shape[0]
    nm = mk.shape[0]
    return pl.pallas_call(
        _mem_attn_prompt_kernel,
        grid=(t // tm,),
        in_specs=[pl.BlockSpec((tm, M_WIDTH), lambda i: (i, 0)),
                  pl.BlockSpec((nm, M_WIDTH), lambda i: (0, 0)),
                  pl.BlockSpec((nm, M_WIDTH), lambda i: (0, 0))],
        out_specs=pl.BlockSpec((tm, M_WIDTH), lambda i: (i, 0)),
        out_shape=jax.ShapeDtypeStruct((t, M_WIDTH), BF16),
        compiler_params=_params(1),
        name="mem_attn_prompt",
    )(qm, mk, mv)


def _mem_attn_sample_kernel(q_ref, k_ref, v_ref, o_ref):
    nm = k_ref.shape[1] // M_HEADS
    for g in range(q_ref.shape[0]):
        outs = []
        for hh in range(M_HEADS):
            rows = pl.ds(hh, nm, stride=M_HEADS)
            s = _nt_dot(q_ref[g, :, hh * HEAD_DIM:(hh + 1) * HEAD_DIM], k_ref[g, rows, :].astype(BF16))
            p = jnp.exp2(s - jnp.max(s, axis=1, keepdims=True))
            pn = (p / jnp.sum(p, axis=1, keepdims=True)).astype(BF16)
            outs.append(_dot(pn, v_ref[g, rows, :].astype(BF16)))
        o_ref[g] = jnp.concatenate(outs, axis=1)


def _mem_attn_sample(q16, mem_k, mem_v):
    nb, nm = mem_k.shape[0], mem_k.shape[1]
    gb = 4 if nb % 4 == 0 else 1
    return pl.pallas_call(
        _mem_attn_sample_kernel,
        grid=(nb // gb,),
        in_specs=[pl.BlockSpec((gb, 16, M_WIDTH), lambda b: (b, 0, 0)),
                  pl.BlockSpec((gb, nm, HEAD_DIM), lambda b: (b, 0, 0)),
                  pl.BlockSpec((gb, nm, HEAD_DIM), lambda b: (b, 0, 0))],
        out_specs=pl.BlockSpec((gb, 16, M_WIDTH), lambda b: (b, 0, 0)),
        out_shape=jax.ShapeDtypeStruct((nb, 16, M_WIDTH), F32),
        compiler_params=_params(1),
        name="mem_attn_sample",
    )(q16, mem_k, mem_v)


def _merged_kernel(h_ref, oa_ref, ob_ref, om_ref, wg0_ref, wg1_ref, wg2_ref, wa_ref, wb_ref, wm_ref, o_ref):
    h = h_ref[...]
    acc = None
    for wg, o, wbr in ((wg0_ref, oa_ref, wa_ref), (wg1_ref, ob_ref, wb_ref), (wg2_ref, om_ref, wm_ref)):
        term = jax.nn.sigmoid(_dot(h, wg[...])) * _dot(o[...].astype(BF16), wbr[...])
        acc = term if acc is None else acc + term
    o_ref[...] = acc.astype(o_ref.dtype)


def _merged(h, o_a, o_b, o_m, w_gates, w_a, w_b, w_m, tm, tn):
    m = h.shape[0]
    nj = D_MODEL // tn
    row = lambda width: pl.BlockSpec((tm, width), lambda j, i: (i, 0))
    gate = lambda b: pl.BlockSpec((D_MODEL, tn), lambda j, i, b=b: (0, b * nj + j))
    br = lambda width: pl.BlockSpec((width, tn), lambda j, i: (0, j))
    return pl.pallas_call(
        _merged_kernel,
        grid=(nj, m // tm),
        in_specs=[row(D_MODEL), row(A_WIDTH), row(R_WIDTH), row(M_WIDTH), gate(0), gate(1), gate(2),
                  br(A_WIDTH), br(R_WIDTH), br(M_WIDTH)],
        out_specs=pl.BlockSpec((tm, tn), lambda j, i: (i, j)),
        out_shape=jax.ShapeDtypeStruct((m, D_MODEL), BF16),
        compiler_params=_params(2),
        name="merged",
    )(h, o_a, o_b, o_m, w_gates, w_gates, w_gates, w_a, w_b, w_m)


def _outproj_kernel(x_ref, mg_ref, wo_ref, n2_ref, wrt_hi_ref, wrt_lo_ref, x2_ref, h2_ref, rl_ref):
    x2 = x_ref[...] + _dot(mg_ref[...], wo_ref[...])
    _store_rows(x2_ref, x2)
    h2 = x2 * lax.rsqrt(jnp.mean(x2 * x2, axis=-1, keepdims=True) + RMS_EPS) * n2_ref[...]
    h2_hi = h2.astype(BF16)
    h2_ref[...] = h2_hi
    h2_lo = (h2 - h2_hi.astype(F32)).astype(BF16)
    rl_ref[...] = (_dot(h2_hi, wrt_hi_ref[...]) + _dot(h2_hi, wrt_lo_ref[...])) + _dot(h2_lo, wrt_hi_ref[...])


def _store_rows(ref, x):
    if len(ref.shape) == 3:
        for j in range(ref.shape[1]):
            ref[:, j, :] = x[:, j * LANES:(j + 1) * LANES]
    else:
        ref[...] = x


def _load_rows(ref):
    if len(ref.shape) == 3:
        return jnp.concatenate([ref[:, j, :] for j in range(ref.shape[1])], axis=1)
    return ref[...]


def _outproj(x, merged, w_out, norm2, w_rt, tm, slabs):
    m = x.shape[0]
    one = pl.Buffered(1)
    row = lambda width: pl.BlockSpec((tm, width), lambda i: (i, 0))
    w_rt_hi = w_rt.astype(BF16)
    w_rt_lo = (w_rt - w_rt_hi.astype(F32)).astype(BF16)
    if slabs:
        x2_spec = pl.BlockSpec((tm, D_MODEL // LANES, LANES), lambda i: (i, 0, 0))
        x2_shape = jax.ShapeDtypeStruct((m, D_MODEL // LANES, LANES), F32)
    else:
        x2_spec, x2_shape = row(D_MODEL), jax.ShapeDtypeStruct((m, D_MODEL), F32)
    return pl.pallas_call(
        _outproj_kernel,
        grid=(m // tm,),
        in_specs=[row(D_MODEL), row(D_MODEL),
                  pl.BlockSpec((D_MODEL, D_MODEL), lambda i: (0, 0), pipeline_mode=one),
                  pl.BlockSpec((1, D_MODEL), lambda i: (0, 0)),
                  pl.BlockSpec((D_MODEL, LANES), lambda i: (0, 0), pipeline_mode=one),
                  pl.BlockSpec((D_MODEL, LANES), lambda i: (0, 0), pipeline_mode=one)],
        out_specs=[x2_spec, row(D_MODEL), row(LANES)],
        out_shape=[x2_shape, jax.ShapeDtypeStruct((m, D_MODEL), BF16), jax.ShapeDtypeStruct((m, LANES), F32)],
        compiler_params=_params(1),
        name="outproj",
    )(x, merged, w_out, norm2.reshape(1, D_MODEL), w_rt_hi, w_rt_lo)


def _router_kernel(rl_ref, comb_ref):
    z = rl_ref[...]
    lane = lax.broadcasted_iota(I32, z.shape, 1).astype(F32)
    ninf = -jnp.inf
    big = 1e9
    is_g = lane < N_GROUPS
    gl = jnp.where(is_g, z, ninf)
    gmax = jnp.max(gl, axis=1, keepdims=True)
    gsum = jnp.sum(jnp.where(is_g, jnp.exp(gl - gmax), 0.0), axis=1, keepdims=True)
    p_sel = 1.0 / gsum
    g_sel = jnp.min(jnp.where(gl == gmax, lane, big), axis=1, keepdims=True)
    lo = N_GROUPS + EXP_PER_GROUP * g_sel
    el = jnp.where(lane >= lo, jnp.where(lane < lo + EXP_PER_GROUP, z, ninf), ninf)
    v1 = jnp.max(el, axis=1, keepdims=True)
    i1 = jnp.min(jnp.where(el == v1, lane, big), axis=1, keepdims=True)
    el2 = jnp.where(lane == i1, ninf, el)
    v2 = jnp.max(el2, axis=1, keepdims=True)
    i2 = jnp.min(jnp.where(el2 == v2, lane, big), axis=1, keepdims=True)
    e2 = jnp.exp(v2 - v1)
    w1 = 1.0 / (1.0 + e2)
    w2 = e2 / (1.0 + e2)
    comb = jnp.where(lane == i1, w1 * p_sel, 0.0) + jnp.where(lane == i2, w2 * p_sel, 0.0)
    comb_ref[...] = jnp.where(lane == 0.0, g_sel, comb)


def _router(rl, tm):
    m = rl.shape[0]
    return pl.pallas_call(
        _router_kernel,
        grid=(m // tm,),
        in_specs=[pl.BlockSpec((tm, LANES), lambda i: (i, 0))],
        out_specs=pl.BlockSpec((tm, LANES), lambda i: (i, 0)),
        out_shape=jax.ShapeDtypeStruct((m, LANES), F32),
        compiler_params=_params(1),
        name="router",
    )(rl)


def _moe_kernel(x2_ref, h2_ref, comb_ref, wg_ref, wu_ref, wd_ref, y_ref, acc_ref, *, ne):
    k = pl.program_id(1)

    @pl.when(k == 0)
    def _():
        acc_ref[...] = x2_ref[...]

    h2 = h2_ref[...]
    comb = comb_ref[...]
    lane = lax.broadcasted_iota(I32, comb.shape, 1)
    for e in range(ne):
        ce = jnp.sum(jnp.where(lane == N_GROUPS + k * ne + e, comb, 0.0), axis=1, keepdims=True)
        a = _dot(h2, wg_ref[e])
        u = _dot(h2, wu_ref[e])
        act = (a * jax.nn.sigmoid(a)) * u * ce
        acc_ref[...] += _dot(act.astype(BF16), wd_ref[e])

    @pl.when(k == pl.num_programs(1) - 1)
    def _():
        y_ref[...] = acc_ref[...]


def _moe(x2, h2, comb, wg, wu, wd, tm, ne):
    m = x2.shape[0]
    row = lambda width: pl.BlockSpec((tm, width), lambda i, k: (i, 0))
    return pl.pallas_call(
        functools.partial(_moe_kernel, ne=ne),
        grid=(m // tm, N_EXPERTS // ne),
        in_specs=[row(D_MODEL), row(D_MODEL), row(LANES),
                  pl.BlockSpec((ne, D_MODEL, EXP_FF), lambda i, k: (k, 0, 0)),
                  pl.BlockSpec((ne, D_MODEL, EXP_FF), lambda i, k: (k, 0, 0)),
                  pl.BlockSpec((ne, EXP_FF, D_MODEL), lambda i, k: (k, 0, 0))],
        out_specs=row(D_MODEL),
        out_shape=jax.ShapeDtypeStruct((m, D_MODEL), F32),
        scratch_shapes=[pltpu.VMEM((tm, D_MODEL), F32)],
        compiler_params=_params(2),
        name="moe",
    )(x2, h2, comb, wg, wu, wd)


def _permute_rows_kernel(idx_ref, *refs, n_arrays, n_init, rows, scatter):
    srcs = refs[:n_arrays]
    dsts = refs[n_arrays + n_init:2 * n_arrays + n_init]
    sems = refs[2 * n_arrays + n_init]
    i = pl.program_id(0)

    def row_copies(src_row, dst_row):
        return [pltpu.make_async_copy(srcs[a].at[pl.ds(src_row, 1)], dsts[a].at[pl.ds(dst_row, 1)], sems.at[a])
                for a in range(n_arrays)]

    def start(r, carry):
        n = i * rows + r
        j = idx_ref[n]
        for cp in (row_copies(n, j) if scatter else row_copies(j, n)):
            cp.start()
        return carry

    def wait(r, carry):
        for cp in row_copies(r, r):
            cp.wait()
        return carry

    lax.fori_loop(0, rows, start, 0)

    @pl.when(i > 0)
    def _():
        lax.fori_loop(0, rows, wait, 0)

    @pl.when(i == pl.num_programs(0) - 1)
    def _():
        lax.fori_loop(0, rows, wait, 0)


def _permute_rows(idx, srcs, out_rows, scatter, name):
    n = idx.shape[0]
    rows = 256
    assert n % rows == 0
    inits = [jnp.zeros((out_rows,) + s.shape[1:], s.dtype) for s in srcs] if scatter else []
    n_arrays, n_init = len(srcs), len(inits)
    any_spec = pl.BlockSpec(memory_space=pl.ANY)
    out_shape = [jax.ShapeDtypeStruct((out_rows,) + s.shape[1:], s.dtype) for s in srcs]
    return pl.pallas_call(
        functools.partial(_permute_rows_kernel, n_arrays=n_arrays, n_init=n_init, rows=rows, scatter=scatter),
        grid_spec=pltpu.PrefetchScalarGridSpec(
            num_scalar_prefetch=1, grid=(n // rows,),
            in_specs=[any_spec] * (n_arrays + n_init), out_specs=[any_spec] * n_arrays,
            scratch_shapes=[pltpu.SemaphoreType.DMA((n_arrays,))]),
        out_shape=out_shape,
        input_output_aliases={1 + n_arrays + a: a for a in range(n_init)},
        compiler_params=_params(1),
        name=name,
    )(idx, *srcs, *inits)


def _moe_grouped_kernel(tg_ref, xs_ref, cs_ref, n2_ref, wg_ref, wu_ref, wd_ref, ys_ref, acc_ref, hn_ref, *, ne):
    t, k = pl.program_id(0), pl.program_id(1)

    @pl.when(k == 0)
    def _():
        x = _load_rows(xs_ref)
        acc_ref[...] = x
        hn = x * lax.rsqrt(jnp.mean(x * x, axis=-1, keepdims=True) + RMS_EPS) * n2_ref[...]
        hn_ref[...] = hn.astype(BF16)

    h2 = hn_ref[...]
    comb = cs_ref[...]
    lane = lax.broadcasted_iota(I32, comb.shape, 1)
    first = N_GROUPS + tg_ref[t] * EXP_PER_GROUP + k * ne
    for e in range(ne):
        ce = jnp.sum(jnp.where(lane == first + e, comb, 0.0), axis=1, keepdims=True)
        a = _dot(h2, wg_ref[e])
        u = _dot(h2, wu_ref[e])
        act = (a * jax.nn.sigmoid(a)) * u * ce
        acc_ref[...] += _dot(act.astype(BF16), wd_ref[e])

    @pl.when(k == pl.num_programs(1) - 1)
    def _():
        _store_rows(ys_ref, acc_ref[...])


def _moe_grouped(x2, comb, norm2, wg, wu, wd):
    n = x2.shape[0]
    slab = x2.shape[1:]
    tile, ne = 512, 2
    steps = EXP_PER_GROUP // ne
    n_tiles = -(-(n + N_GROUPS * (tile - 1)) // tile)
    g = comb[:, 0].astype(I32)
    onehot = (g[:, None] == jnp.arange(N_GROUPS, dtype=I32)[None, :]).astype(I32)
    rank = jnp.sum((jnp.cumsum(onehot, axis=0) - onehot) * onehot, axis=1)
    padded = (jnp.sum(onehot, axis=0) + tile - 1) // tile * tile
    ends = jnp.cumsum(padded)
    dest = (jnp.sum(onehot * (ends - padded)[None, :], axis=1) + rank).astype(I32)
    tile_group = jnp.minimum(jnp.sum((jnp.arange(n_tiles, dtype=I32) * tile)[:, None] >= ends[None, :], axis=1),
                             N_GROUPS - 1).astype(I32)

    xs, cs = _permute_rows(dest, [x2, comb], n_tiles * tile, True, "moe_dispatch")
    row = lambda width: pl.BlockSpec((tile, width), lambda t, k, tg: (t, 0))
    slab_spec = pl.BlockSpec((tile,) + slab, lambda t, k, tg: (t, 0, 0))
    wspec = lambda shape: pl.BlockSpec((ne,) + shape, lambda t, k, tg: (tg[t] * steps + k, 0, 0))
    ys = pl.pallas_call(
        functools.partial(_moe_grouped_kernel, ne=ne),
        grid_spec=pltpu.PrefetchScalarGridSpec(
            num_scalar_prefetch=1, grid=(n_tiles, steps),
            in_specs=[slab_spec, row(LANES), pl.BlockSpec((1, D_MODEL), lambda t, k, tg: (0, 0)),
                      wspec((D_MODEL, EXP_FF)), wspec((D_MODEL, EXP_FF)), wspec((EXP_FF, D_MODEL))],
            out_specs=slab_spec,
            scratch_shapes=[pltpu.VMEM((tile, D_MODEL), F32), pltpu.VMEM((tile, D_MODEL), BF16)]),
        out_shape=jax.ShapeDtypeStruct((n_tiles * tile,) + slab, F32),
        compiler_params=_params(2),
        name="moe_grouped",
    )(tile_group, xs, cs, norm2.reshape(1, D_MODEL), wg, wu, wd)
    (y,) = _permute_rows(dest, [ys], n, False, "moe_collect")
    return y.reshape(n, D_MODEL)


def _rope_tables(pos):
    half = R_DK // 2
    freqs = ROPE_BASE ** (-jnp.arange(half, dtype=F32) / half)
    ang = pos.astype(F32)[:, None] * freqs[None, :]
    cos, sin = jnp.cos(ang), jnp.sin(ang)
    return jnp.concatenate([cos, cos], axis=1), jnp.concatenate([-sin, sin], axis=1)


def _layer_weights(l, w_in, w_mem_kv, w_br_a, w_br_b, w_br_m, w_out, w_group, w_router,
                   w_gate_e, w_up_e, w_down_e):
    wi = w_in[l]
    col = lambda a, b: wi[:, a:b].astype(BF16)
    w_kw = jnp.pad(wi[:, O_KI:O_QR], ((0, 0), (0, LANES - (O_QR - O_KI)))).astype(BF16)
    w_rt = jnp.pad(jnp.concatenate([w_group[l], w_router[l]], axis=1),
                   ((0, 0), (0, LANES - N_GROUPS - N_EXPERTS)))
    return dict(
        qa=col(O_QA, O_KA), ka=col(O_KA, O_VA), va=col(O_VA, O_QI), qi=col(O_QI, O_KI), kw=w_kw,
        qr=col(O_QR, O_KR), kr=col(O_KR, O_VR), vg=col(O_VR, O_QM), qm=col(O_QM, O_GATES),
        gates=col(O_GATES, O_END),
        mk=w_mem_kv[l][:, :M_WIDTH].astype(BF16), mv=w_mem_kv[l][:, M_WIDTH:].astype(BF16),
        br_a=w_br_a[l].astype(BF16), br_b=w_br_b[l].astype(BF16), br_m=w_br_m[l].astype(BF16),
        out=w_out[l].astype(BF16), rt=w_rt,
        ge=w_gate_e[l].astype(BF16), ue=w_up_e[l].astype(BF16), de=w_down_e[l].astype(BF16))


def _project(x2d, pos, w, norms, tm):
    t = x2d.shape[0]
    norm1, q_norm_a, k_norm_a, idx_k_norm, q_norm_m = norms
    h = _rmsnorm_cast(x2d, norm1, tm)
    hd = lambda g: g.reshape(1, HEAD_DIM)
    gspec = _const_spec((1, HEAD_DIM))
    scale = HEAD_DIM ** -0.5 * LOG2_E
    (qa,) = _proj(functools.partial(_proj_headnorm_kernel, scale=scale), h, w["qa"], [hd(q_norm_a)], [gspec],
                  [(BF16, ROW)], tm, A_WIDTH, "proj_qa")
    ka, kab = _proj(functools.partial(_proj_headnorm_kernel, scale=1.0), h, w["ka"], [hd(k_norm_a)], [gspec],
                    [(F32, HEAD_MAJOR), (BF16, HEAD_MAJOR)], tm, A_WIDTH, "proj_ka")
    va, vab = _proj(_proj_plain_kernel, h, w["va"], [], [], [(F32, HEAD_MAJOR), (BF16, HEAD_MAJOR)], tm, A_WIDTH,
                    "proj_va")
    (qi,) = _proj(_proj_plain_kernel, h, w["qi"], [], [], [(BF16, ROW)], tm, 512, "proj_qi")
    g_idx = jnp.pad(idx_k_norm, (0, LANES - IDX_DIM)).reshape(1, LANES)
    kw, kt, kte, kto = _proj(_proj_kw_kernel, h, w["kw"], [g_idx], [_const_spec((1, LANES))],
                             [(F32, ROW), (F32, TRANSPOSED), (BF16, TRANSPOSED), (BF16, TRANSPOSED)], tm, LANES,
                             "proj_kw")
    cos2, sin2 = _rope_tables(pos)
    tabs, tab_specs = [cos2, sin2], [_row_spec(tm, R_DK), _row_spec(tm, R_DK)]
    (qr,) = _proj(functools.partial(_proj_rope_kernel, scale=1.0), h, w["qr"], tabs, tab_specs,
                  [(F32, ROW)], tm, R_WIDTH, "proj_qr")
    (kr,) = _proj(functools.partial(_proj_rope_kernel, scale=R_DK ** -0.5), h, w["kr"], tabs, tab_specs,
                  [(F32, ROW)], tm, R_WIDTH, "proj_kr")
    (vg,) = _proj(_proj_plain_kernel, h, w["vg"], [], [], [(F32, ROW)], tm, R_WIDTH, "proj_vg")
    (qm,) = _proj(functools.partial(_proj_headnorm_kernel, scale=scale), h, w["qm"], [hd(q_norm_m)], [gspec],
                  [(BF16, ROW)], tm, M_WIDTH, "proj_qm")
    return dict(h=h, qa=qa, ka=ka, kab=kab, va=va, vab=vab, qi=qi, kw=kw, kt=kt, kte=kte, kto=kto,
                qr=qr, kr=kr, vg=vg, qm=qm)


def _finish(x2d, h, o_a, o_b, o_m, w, norm2, tm):
    merged = _merged(h, o_a, o_b, o_m, w["gates"], w["br_a"], w["br_b"], w["br_m"], tm, 512)
    grouped = x2d.shape[0] >= 4 * 512
    x2, h2, rl = _outproj(x2d, merged, w["out"], norm2, w["rt"], min(tm, 256), grouped)
    comb = _router(rl, tm)
    if grouped:
        return _moe_grouped(x2, comb, norm2, w["ge"], w["ue"], w["de"])
    return _moe(x2, h2, comb, w["ge"], w["ue"], w["de"], tm, 2)


def kernel(x_prompt, x_sample, mem_prompt, cache_k, cache_v, cache_idx_k, state_ret, cache_mem_k, cache_mem_v,
           page_table, norm1, w_in, q_norm_a, k_norm_a, idx_k_norm, q_norm_m, k_norm_m, mem_norm, w_mem_kv,
           w_br_a, w_br_b, w_br_m, w_out, norm2, w_group, w_router, w_gate_e, w_up_e, w_down_e):
    depth = w_in.shape[0]
    bp, seq, _ = x_prompt.shape
    nb, n_tok, _ = x_sample.shape
    assert bp == 1 and n_tok <= 8
    y_p = x_prompt.reshape(seq, D_MODEL)
    y_s = x_sample.reshape(nb * n_tok, D_MODEL)
    pos_p = jnp.arange(seq)
    pos_s = jnp.tile(PAST_LEN + jnp.arange(n_tok), nb)
    outs = [[] for _ in range(10)]
    for l in range(depth):
        w = _layer_weights(l, w_in, w_mem_kv, w_br_a, w_br_b, w_br_m, w_out, w_group, w_router,
                           w_gate_e, w_up_e, w_down_e)
        norms = (norm1[l], q_norm_a[l], k_norm_a[l], idx_k_norm[l], q_norm_m[l])

        tm = 512
        p = _project(y_p, pos_p, w, norms, tm)
        o_a = _dsa_prompt(p["qa"], p["qi"], p["kw"], p["kte"], p["kto"], p["kab"], p["vab"])
        o_r, s_fin = _retention_prompt(p["qr"], p["kr"], p["vg"])
        n_mem = mem_prompt.shape[1]
        hm = _rmsnorm_cast(mem_prompt.reshape(n_mem, D_MODEL), mem_norm[l], n_mem)
        mk, mkb = _proj(functools.partial(_proj_headnorm_kernel, scale=1.0), hm, w["mk"],
                        [k_norm_m[l].reshape(1, HEAD_DIM)], [_const_spec((1, HEAD_DIM))],
                        [(F32, ROW), (BF16, ROW)], n_mem, M_WIDTH, "proj_mk")
        mv, mvb = _proj(_proj_plain_kernel, hm, w["mv"], [], [], [(F32, ROW), (BF16, ROW)],
                        n_mem, M_WIDTH, "proj_mv")
        o_m = _mem_attn_prompt(p["qm"], mkb, mvb, tm)
        y_p = _finish(y_p, p["h"], o_a, o_r, o_m, w, norm2[l], tm)
        outs[0].append(jnp.transpose(p["ka"], (1, 0, 2)).reshape(bp, seq, A_HEADS, HEAD_DIM))
        outs[1].append(jnp.transpose(p["va"], (1, 0, 2)).reshape(bp, seq, A_HEADS, HEAD_DIM))
        outs[2].append(p["kt"][:IDX_DIM].T.reshape(bp, seq, IDX_DIM))
        outs[3].append(s_fin.reshape(bp, R_HEADS, R_DK, 128))
        outs[4].append(mk.reshape(bp, n_mem, M_HEADS, HEAD_DIM))
        outs[5].append(mv.reshape(bp, n_mem, M_HEADS, HEAD_DIM))

        ts = nb * n_tok
        s = _project(y_s, pos_s, w, norms, ts)
        pad_tok = lambda a: jnp.pad(a.reshape(nb, n_tok, -1), ((0, 0), (0, 8 - n_tok), (0, 0)))
        a_idx = s["qi"].reshape(nb, n_tok * IDX_HEADS, IDX_DIM)
        w_idx = s["kw"][:, IDX_DIM:IDX_DIM + IDX_HEADS].reshape(nb, n_tok * IDX_HEADS, 1)
        pad16 = lambda a: jnp.pad(a.reshape(nb, n_tok, -1), ((0, 0), (0, 16 - n_tok), (0, 0)))
        ki_new = jnp.pad(jnp.transpose(s["kte"][:IDX_DIM].reshape(IDX_DIM, nb, n_tok), (1, 0, 2)),
                         ((0, 0), (0, 0), (0, PAGE_SIZE - n_tok)))
        new_page = lambda a: jnp.pad(jnp.transpose(a.reshape(A_HEADS, nb, n_tok, HEAD_DIM), (1, 0, 2, 3)),
                                     ((0, 0), (0, 0), (0, PAGE_SIZE - n_tok), (0, 0)))
        head_major_pool = lambda c: jnp.transpose(c, (0, 2, 1, 3))
        o_a_s = _dsa_sample(page_table, a_idx, w_idx, pad16(s["qa"]),
                            jnp.swapaxes(cache_idx_k[l], 1, 2), ki_new,
                            head_major_pool(cache_k[l]), new_page(s["kab"]),
                            head_major_pool(cache_v[l]), new_page(s["vab"]), n_tok)
        o_r_s, s_new = _retention_sample(pad_tok(s["qr"]), pad_tok(s["kr"]), pad_tok(s["vg"][:, :R_WIDTH]),
                                         pad_tok(s["vg"][:, R_WIDTH:]), state_ret[l], n_tok)
        mem_rows = lambda c: c.reshape(nb, -1, HEAD_DIM)
        o_m_s = _mem_attn_sample(pad16(s["qm"]), mem_rows(cache_mem_k[l]), mem_rows(cache_mem_v[l]))
        unpad = lambda a: a[:, :n_tok, :].reshape(ts, -1)
        y_s = _finish(y_s, s["h"], unpad(o_a_s), unpad(o_r_s), unpad(o_m_s), w, norm2[l], ts)
        outs[6].append(jnp.transpose(s["ka"], (1, 0, 2)).reshape(nb, n_tok, A_HEADS, HEAD_DIM))
        outs[7].append(jnp.transpose(s["va"], (1, 0, 2)).reshape(nb, n_tok, A_HEADS, HEAD_DIM))
        outs[8].append(s["kt"][:IDX_DIM].T.reshape(nb, n_tok, IDX_DIM))
        outs[9].append(s_new)
    stk = [jnp.stack(o) for o in outs]
    return (y_p.reshape(bp, seq, D_MODEL), y_s.reshape(nb, n_tok, D_MODEL), *stk)
```

```python
import functools

import jax
import jax.numpy as jnp
import numpy as np
from jax import lax
from jax.experimental import pallas as pl
from jax.experimental.pallas import tpu as pltpu

F32 = jnp.float32
BF16 = jnp.bfloat16
I32 = jnp.int32

D_MODEL = 2048
PAST_LEN = 2048
PAGE_SIZE = 128
N_PAGES = PAST_LEN // PAGE_SIZE
HEAD_DIM = 128
A_HEADS = 6
A_WIDTH = A_HEADS * HEAD_DIM
IDX_HEADS = 16
IDX_DIM = 64
TOPK_MAX = 256
R_HEADS = 6
R_DK = 128
R_WIDTH = R_HEADS * 128
RET_CHUNK = 128
ROPE_BASE = 10000.0
M_HEADS = 4
M_WIDTH = M_HEADS * HEAD_DIM
N_BRANCH = 3
N_GROUPS = 4
EXP_PER_GROUP = 8
N_EXPERTS = N_GROUPS * EXP_PER_GROUP
EXP_FF = 256
RMS_EPS = 1e-6

LANES = 128
INT_MIN = -(2 ** 31)
NEG_BIG = -1e30
LOG2_E = 1.4426950408889634
V7X_VMEM_LIMIT = 56 * 1024 * 1024

_SIZES = (A_WIDTH, A_WIDTH, A_WIDTH, IDX_HEADS * IDX_DIM, IDX_DIM, IDX_HEADS,
          R_HEADS * R_DK, R_HEADS * R_DK, R_WIDTH, R_WIDTH, M_WIDTH, N_BRANCH * D_MODEL)
_OFFS = tuple(int(v) for v in np.cumsum((0,) + _SIZES))
(O_QA, O_KA, O_VA, O_QI, O_KI, O_WI, O_QR, O_KR, O_VR, O_GR, O_QM, O_GATES, O_END) = _OFFS


def _params(n_grid, vmem=V7X_VMEM_LIMIT):
    return pltpu.CompilerParams(dimension_semantics=("arbitrary",) * n_grid, vmem_limit_bytes=vmem)


def _nt_dot(a, b):
    return lax.dot_general(a, b, (((1,), (1,)), ((), ())), preferred_element_type=F32)


def _dot(a, b):
    return jnp.dot(a, b, preferred_element_type=F32)


def _rmsnorm_kernel(x_ref, g_ref, o_ref):
    x = x_ref[...]
    ms = jnp.mean(x * x, axis=-1, keepdims=True)
    o_ref[...] = (x * lax.rsqrt(ms + RMS_EPS) * g_ref[...]).astype(o_ref.dtype)


def _rmsnorm_cast(x, g, tm):
    m, d = x.shape
    return pl.pallas_call(
        _rmsnorm_kernel,
        grid=(m // tm,),
        in_specs=[pl.BlockSpec((tm, d), lambda i: (i, 0)), pl.BlockSpec((1, d), lambda i: (0, 0))],
        out_specs=pl.BlockSpec((tm, d), lambda i: (i, 0)),
        out_shape=jax.ShapeDtypeStruct((m, d), BF16),
        compiler_params=_params(1),
        name="rmsnorm_cast",
    )(x, g.reshape(1, d))


def _store_cols(o_ref, hh, z):
    if len(o_ref.shape) == 3:
        o_ref[hh] = z.astype(o_ref.dtype)
    else:
        o_ref[:, hh * HEAD_DIM:(hh + 1) * HEAD_DIM] = z.astype(o_ref.dtype)


def _proj_plain_kernel(h_ref, w_ref, *o_refs):
    y = _dot(h_ref[...], w_ref[...])
    for o in o_refs:
        if len(o.shape) == 3:
            for hh in range(o.shape[0]):
                o[hh] = y[:, hh * HEAD_DIM:(hh + 1) * HEAD_DIM].astype(o.dtype)
        else:
            o[...] = y.astype(o.dtype)


def _proj_headnorm_kernel(h_ref, w_ref, g_ref, *o_refs, scale):
    y = _dot(h_ref[...], w_ref[...])
    g = g_ref[...]
    for hh in range(y.shape[1] // HEAD_DIM):
        sl = slice(hh * HEAD_DIM, (hh + 1) * HEAD_DIM)
        yh = y[:, sl]
        ms = jnp.mean(yh * yh, axis=-1, keepdims=True)
        z = yh * lax.rsqrt(ms + RMS_EPS) * g
        if scale != 1.0:
            z = z * scale
        for o in o_refs:
            _store_cols(o, hh, z)


def _proj_rope_kernel(h_ref, w_ref, cos_ref, sin_ref, o_ref, *, scale):
    y = _dot(h_ref[...], w_ref[...])
    c = cos_ref[...]
    s = sin_ref[...]
    for hh in range(y.shape[1] // R_DK):
        sl = slice(hh * R_DK, (hh + 1) * R_DK)
        yh = y[:, sl]
        z = yh * c + pltpu.roll(yh, R_DK // 2, 1) * s
        if scale != 1.0:
            z = z * scale
        o_ref[:, sl] = z


def _proj_kw_kernel(h_ref, w_ref, g_ref, kw_ref, kt_ref, kte_ref, kto_ref):
    y = _dot(h_ref[...], w_ref[...])
    lane = lax.broadcasted_iota(I32, y.shape, 1)
    is_k = lane < IDX_DIM
    ms = jnp.sum(jnp.where(is_k, y * y, 0.0), axis=-1, keepdims=True) * (1.0 / IDX_DIM)
    kn = y * lax.rsqrt(ms + RMS_EPS) * g_ref[...]
    kw_ref[...] = jnp.where(is_k, kn, y * (IDX_HEADS * IDX_DIM) ** -0.5)
    et = jnp.where(is_k, kn, 0.0).T
    kt_ref[...] = et
    kte_ref[...] = et.astype(BF16)
    kto_ref[...] = pltpu.roll(et, IDX_DIM, 0).astype(BF16)


ROW, TRANSPOSED, HEAD_MAJOR = "row", "transposed", "head_major"


def _proj(kern, h, w, extras, extra_specs, outs, tm, tn, name):
    m, k = h.shape
    n = w.shape[1]
    out_shape, out_specs = [], []
    for dt, layout in outs:
        if layout == TRANSPOSED:
            out_shape.append(jax.ShapeDtypeStruct((n, m), dt))
            out_specs.append(pl.BlockSpec((tn, tm), lambda j, i: (j, i)))
        elif layout == HEAD_MAJOR:
            out_shape.append(jax.ShapeDtypeStruct((n // HEAD_DIM, m, HEAD_DIM), dt))
            out_specs.append(pl.BlockSpec((tn // HEAD_DIM, tm, HEAD_DIM), lambda j, i: (j, i, 0)))
        else:
            out_shape.append(jax.ShapeDtypeStruct((m, n), dt))
            out_specs.append(pl.BlockSpec((tm, tn), lambda j, i: (i, j)))
    res = pl.pallas_call(
        kern,
        grid=(n // tn, m // tm),
        in_specs=[pl.BlockSpec((tm, k), lambda j, i: (i, 0)),
                  pl.BlockSpec((k, tn), lambda j, i: (0, j))] + list(extra_specs),
        out_specs=out_specs,
        out_shape=out_shape,
        compiler_params=_params(2),
        name=name,
    )(h, w, *extras)
    return res


def _row_spec(tm, width):
    return pl.BlockSpec((tm, width), lambda j, i: (i, 0))


def _const_spec(shape):
    return pl.BlockSpec(shape, lambda j, i: (0,) * len(shape))


def _score_key(score):
    bits = lax.bitcast_convert_type(score, I32)
    return jnp.where(bits < 0, bits ^ jnp.int32(0x7FFFFFFF), bits)


def _bisect_threshold(count_ge, rows, topk):
    def bit_body(it, carry):
        ua, cnt = carry
        cand_u = ua | jnp.left_shift(jnp.int32(1), 31 - it)
        total = count_ge(cand_u ^ INT_MIN)
        ok = total >= topk
        return jnp.where(ok, cand_u, ua), jnp.where(ok, jnp.broadcast_to(total, cnt.shape), cnt)

    init = (jnp.zeros((rows, LANES), I32), jnp.zeros((rows, LANES), F32))
    ua, cnt = lax.fori_loop(0, 32, bit_body, init)
    return jnp.maximum(ua ^ INT_MIN, INT_MIN + 1), cnt


def _demote_excess_ties(keys_ref, thr, need, n_pieces):
    rows = keys_ref.shape[0]
    tri = jnp.where(lax.broadcasted_iota(I32, (LANES, LANES), 0) <= lax.broadcasted_iota(I32, (LANES, LANES), 1),
                    1.0, 0.0).astype(BF16)

    def piece(j, seen):
        off = pl.multiple_of(j * LANES, LANES)
        k = keys_ref[:, pl.ds(off, LANES)]
        tie = jnp.where(k == thr, 1.0, 0.0)
        rank = seen + _dot(tie.astype(BF16), tri)
        keys_ref[:, pl.ds(off, LANES)] = jnp.where(tie * rank > need, INT_MIN, k)
        return jnp.broadcast_to(rank[:, LANES - 1:LANES], (rows, LANES))

    lax.fori_loop(0, n_pieces, piece, jnp.zeros((rows, LANES), F32))


def _dsa_prompt_kernel(qa_ref, qi_ref, kw_ref, kte_ref, kto_ref, k_ref, v_ref, o_ref,
                       keys_ref, wb_ref, m_ref, l_ref, acc_ref, a_ref, s_ref, p_ref, *, topk, tq, ts, tc, tk):
    i = pl.program_id(0)
    n_vis = i * tq + tq
    n_sc = (n_vis + ts - 1) // ts
    n_ch = (n_vis + tk - 1) // tk
    kw = kw_ref[...]
    for hh in range(IDX_HEADS):
        wb_ref[hh] = jnp.broadcast_to(kw[:, IDX_DIM + hh:IDX_DIM + hh + 1], (tq, LANES))
    row = i * tq + lax.broadcasted_iota(I32, (tq, LANES), 0)
    lane = lax.broadcasted_iota(I32, (tq, LANES), 1)

    def score_chunk(c, carry):
        off = pl.multiple_of(c * ts, ts)
        kte = kte_ref[:, pl.ds(off, ts)]
        kto = kto_ref[:, pl.ds(off, ts)]
        acc = [jnp.zeros((tq, LANES), F32) for _ in range(ts // LANES)]
        for p in range(IDX_HEADS // 2):
            lhs = qi_ref[:, p * LANES:(p + 1) * LANES]
            for w_head, rhs in ((wb_ref[2 * p], kte), (wb_ref[2 * p + 1], kto)):
                r = jnp.maximum(_dot(lhs, rhs), 0.0)
                for u in range(ts // LANES):
                    acc[u] = acc[u] + w_head * r[:, u * LANES:(u + 1) * LANES]
        for u in range(ts // LANES):
            col = off + u * LANES + lane
            keys_ref[:, pl.ds(off + u * LANES, LANES)] = jnp.where(col <= row, _score_key(acc[u]), INT_MIN)
        return carry

    lax.fori_loop(0, n_sc, score_chunk, 0)

    n_cc = (n_vis + tc - 1) // tc

    def pad_chunk(c, carry):
        keys_ref[:, pl.ds(pl.multiple_of(c * ts, ts), ts)] = jnp.full((tq, ts), INT_MIN, I32)
        return carry

    lax.fori_loop(n_sc, n_cc * (tc // ts), pad_chunk, 0)

    def count(pred):
        def count_chunk(c, cnt):
            off = pl.multiple_of(c * tc, tc)
            for u in range(tc // LANES):
                cnt = cnt + jnp.where(pred(keys_ref[:, pl.ds(off + u * LANES, LANES)]), 1.0, 0.0)
            return cnt

        cnt = lax.fori_loop(0, n_cc, count_chunk, jnp.zeros((tq, LANES), F32))
        return jnp.sum(cnt, axis=1, keepdims=True)

    thr, cnt_ge = _bisect_threshold(lambda cand: count(lambda k: k >= cand), tq, topk)

    @pl.when(jnp.max(cnt_ge) > topk)
    def _():
        need = topk - count(lambda k: k > thr)
        _demote_excess_ties(keys_ref, thr, need, n_cc * (tc // LANES))

    m_ref[...] = jnp.full(m_ref.shape, NEG_BIG, F32)
    l_ref[...] = jnp.zeros(l_ref.shape, F32)
    acc_ref[...] = jnp.zeros(acc_ref.shape, F32)

    def att_chunk(c, carry):
        off = pl.multiple_of(c * tk, tk)
        n_u = tk // LANES
        bias = jnp.concatenate(
            [jnp.where(keys_ref[:, pl.ds(off + u * LANES, LANES)] >= thr, 0.0, NEG_BIG) for u in range(n_u)], axis=1)
        half = tk // 2
        for hh in range(A_HEADS):
            q = qa_ref[:, hh * HEAD_DIM:(hh + 1) * HEAD_DIM]
            for part in range(2):
                s_ref[hh, :, part * half:(part + 1) * half] = (
                    _nt_dot(q, k_ref[hh, pl.ds(off + part * half, half), :]) + bias[:, part * half:(part + 1) * half])
        for hh in range(A_HEADS):
            m_old = m_ref[hh]
            mx = s_ref[hh, :, 0:LANES]
            for u in range(1, n_u):
                mx = jnp.maximum(mx, s_ref[hh, :, u * LANES:(u + 1) * LANES])
            m_new = jnp.maximum(m_old, jnp.broadcast_to(jnp.max(mx, axis=1, keepdims=True), (tq, LANES)))
            alpha = jnp.exp2(m_old - m_new)
            psum = jnp.zeros((tq, LANES), F32)
            for u in range(n_u):
                p = jnp.exp2(s_ref[hh, :, u * LANES:(u + 1) * LANES] - m_new)
                psum = psum + p
                p_ref[hh, :, u * LANES:(u + 1) * LANES] = p.astype(BF16)
            l_ref[hh] = alpha * l_ref[hh] + jnp.broadcast_to(jnp.sum(psum, axis=1, keepdims=True), (tq, LANES))
            m_ref[hh] = m_new
            a_ref[hh] = alpha
        for hh in range(A_HEADS):
            acc_ref[hh] = a_ref[hh] * acc_ref[hh] + _dot(p_ref[hh], v_ref[hh, pl.ds(off, tk), :])
        return carry

    lax.fori_loop(0, n_ch, att_chunk, 0)
    for hh in range(A_HEADS):
        o_ref[:, hh * HEAD_DIM:(hh + 1) * HEAD_DIM] = (acc_ref[hh] / l_ref[hh]).astype(o_ref.dtype)


def _dsa_prompt(qa, qi, kw, kte, kto, kb, vb):
    t = qa.shape[0]
    tq, ts, tk = 128, 256, 512
    tc = 1024 if t % 1024 == 0 else tk
    assert t % tk == 0 and tc % tk == 0 and tk % ts == 0
    topk = min(TOPK_MAX, t // 4)
    one = pl.Buffered(1)
    head_state = pltpu.VMEM((A_HEADS, tq, LANES), F32)
    return pl.pallas_call(
        functools.partial(_dsa_prompt_kernel, topk=topk, tq=tq, ts=ts, tc=tc, tk=tk),
        grid=(t // tq,),
        in_specs=[pl.BlockSpec((tq, A_WIDTH), lambda i: (i, 0)),
                  pl.BlockSpec((tq, IDX_HEADS * IDX_DIM), lambda i: (i, 0)),
                  pl.BlockSpec((tq, LANES), lambda i: (i, 0)),
                  pl.BlockSpec((LANES, t), lambda i: (0, 0), pipeline_mode=one),
                  pl.BlockSpec((LANES, t), lambda i: (0, 0), pipeline_mode=one),
                  pl.BlockSpec((A_HEADS, t, HEAD_DIM), lambda i: (0, 0, 0), pipeline_mode=one),
                  pl.BlockSpec((A_HEADS, t, HEAD_DIM), lambda i: (0, 0, 0), pipeline_mode=one)],
        out_specs=pl.BlockSpec((tq, A_WIDTH), lambda i: (i, 0)),
        out_shape=jax.ShapeDtypeStruct((t, A_WIDTH), BF16),
        scratch_shapes=[pltpu.VMEM((tq, t), I32), pltpu.VMEM((IDX_HEADS, tq, LANES), F32),
                        head_state, head_state, head_state, head_state,
                        pltpu.VMEM((A_HEADS, tq, tk), F32), pltpu.VMEM((A_HEADS, tq, tk), BF16)],
        compiler_params=_params(1),
        name="dsa_prompt",
    )(qa, qi, kw, kte, kto, kb, vb)


def _dsa_sample_scores_kernel(pt_ref, a_ref, w_ref, *refs, n_tok):
    idx_pages, ki_new, keys_ref = refs[:N_PAGES], refs[N_PAGES], refs[N_PAGES + 1]
    a = a_ref[...]
    wcol = jnp.broadcast_to(w_ref[...], (n_tok * IDX_HEADS, LANES))
    sub8 = lax.broadcasted_iota(I32, (8, LANES), 0)
    lane8 = lax.broadcasted_iota(I32, (8, LANES), 1)
    for j in range(N_PAGES + 1):
        kp = idx_pages[j][...].astype(BF16) if j < N_PAGES else ki_new[...]
        r = jnp.maximum(_dot(a, kp), 0.0) * wcol
        k8 = jnp.full((8, LANES), INT_MIN, I32)
        for t in range(n_tok):
            st = jnp.sum(r[t * IDX_HEADS:(t + 1) * IDX_HEADS, :], axis=0, keepdims=True)
            kt = jnp.broadcast_to(_score_key(st), (8, LANES))
            if j == N_PAGES:
                kt = jnp.where(lane8 <= t, kt, INT_MIN)
            k8 = jnp.where(sub8 == t, kt, k8)
        keys_ref[:, j * LANES:(j + 1) * LANES] = k8


def _threshold_kernel(keys_ref, thr_ref, kout_ref, *, topk):
    rows, n_keys = keys_ref.shape
    kout_ref[...] = keys_ref[...]

    def count(pred):
        cnt = jnp.zeros((rows, LANES), F32)
        for j in range(n_keys // LANES):
            cnt = cnt + jnp.where(pred(keys_ref[:, j * LANES:(j + 1) * LANES]), 1.0, 0.0)
        return jnp.sum(cnt, axis=1, keepdims=True)

    thr, cnt_ge = _bisect_threshold(lambda cand: count(lambda k: k >= cand), rows, topk)
    thr_ref[...] = thr

    @pl.when(jnp.max(cnt_ge) > topk)
    def _():
        need = topk - count(lambda k: k > thr)
        _demote_excess_ties(kout_ref, thr, need, n_keys // LANES)


def _dsa_sample_attend_kernel(pt_ref, q_ref, keys_ref, thr_ref, *refs):
    n_pg = N_PAGES
    k_pages, k_new = refs[:n_pg], refs[n_pg]
    v_pages, v_new = refs[n_pg + 1:2 * n_pg + 1], refs[2 * n_pg + 1]
    o_ref, lg_ref, pn_ref = refs[2 * n_pg + 2:]
    thr = thr_ref[...]
    for j in range(n_pg + 1):
        page = k_pages[j] if j < n_pg else k_new
        bias8 = jnp.where(keys_ref[:, j * LANES:(j + 1) * LANES] >= thr, 0.0, NEG_BIG)
        bias = jnp.concatenate([bias8, bias8], axis=0)
        for hh in range(A_HEADS):
            s = _nt_dot(q_ref[:, hh * HEAD_DIM:(hh + 1) * HEAD_DIM], page[hh].astype(BF16))
            lg_ref[hh, :, j * LANES:(j + 1) * LANES] = s + bias
    for hh in range(A_HEADS):
        lg = lg_ref[hh]
        p = jnp.exp2(lg - jnp.max(lg, axis=1, keepdims=True))
        pn_ref[hh] = (p / jnp.sum(p, axis=1, keepdims=True)).astype(BF16)
    acc = [jnp.zeros((16, HEAD_DIM), F32) for _ in range(A_HEADS)]
    for j in range(n_pg + 1):
        page = v_pages[j] if j < n_pg else v_new
        for hh in range(A_HEADS):
            acc[hh] = acc[hh] + _dot(pn_ref[hh, :, j * LANES:(j + 1) * LANES], page[hh].astype(BF16))
    o_ref[...] = jnp.concatenate(acc, axis=1)


def _dsa_sample(page_table, a, wcol, q16, idx_pool, ki_new, k_pool, k_new, v_pool, v_new, n_tok):
    nb = a.shape[0]
    topk = min(TOPK_MAX, (PAST_LEN + n_tok) // 4)
    n_keys = (N_PAGES + 1) * LANES

    def batch_spec(rows, width):
        return pl.BlockSpec((None, rows, width), lambda b, pt: (b, 0, 0))

    idx_page = lambda j: pl.BlockSpec((None, IDX_DIM, PAGE_SIZE), lambda b, pt, j=j: (pt[b, j], 0, 0))
    keys = pl.pallas_call(
        functools.partial(_dsa_sample_scores_kernel, n_tok=n_tok),
        grid_spec=pltpu.PrefetchScalarGridSpec(
            num_scalar_prefetch=1, grid=(nb,),
            in_specs=[batch_spec(n_tok * IDX_HEADS, IDX_DIM), batch_spec(n_tok * IDX_HEADS, 1)]
            + [idx_page(j) for j in range(N_PAGES)] + [batch_spec(IDX_DIM, PAGE_SIZE)],
            out_specs=batch_spec(8, n_keys)),
        out_shape=jax.ShapeDtypeStruct((nb, 8, n_keys), I32),
        compiler_params=_params(1),
        name="dsa_sample_scores",
    )(page_table, a, wcol, *([idx_pool] * N_PAGES), ki_new)

    rows = nb * 8
    tr = min(rows, 256)
    thr, keys = pl.pallas_call(
        functools.partial(_threshold_kernel, topk=topk),
        grid=(rows // tr,),
        in_specs=[pl.BlockSpec((tr, n_keys), lambda i: (i, 0))],
        out_specs=[pl.BlockSpec((tr, LANES), lambda i: (i, 0)), pl.BlockSpec((tr, n_keys), lambda i: (i, 0))],
        out_shape=[jax.ShapeDtypeStruct((rows, LANES), I32), jax.ShapeDtypeStruct((rows, n_keys), I32)],
        compiler_params=_params(1),
        name="dsa_sample_threshold",
    )(keys.reshape(rows, n_keys))
    thr, keys = thr.reshape(nb, 8, LANES), keys.reshape(nb, 8, n_keys)

    kv_page = lambda j: pl.BlockSpec((None, A_HEADS, PAGE_SIZE, HEAD_DIM), lambda b, pt, j=j: (pt[b, j], 0, 0, 0))
    kv_new = pl.BlockSpec((None, A_HEADS, PAGE_SIZE, HEAD_DIM), lambda b, pt: (b, 0, 0, 0))
    return pl.pallas_call(
        _dsa_sample_attend_kernel,
        grid_spec=pltpu.PrefetchScalarGridSpec(
            num_scalar_prefetch=1, grid=(nb,),
            in_specs=[batch_spec(16, A_WIDTH), batch_spec(8, n_keys), batch_spec(8, LANES)]
            + [kv_page(j) for j in range(N_PAGES)] + [kv_new] + [kv_page(j) for j in range(N_PAGES)] + [kv_new],
            out_specs=batch_spec(16, A_WIDTH),
            scratch_shapes=[pltpu.VMEM((A_HEADS, 16, n_keys), F32), pltpu.VMEM((A_HEADS, 16, n_keys), BF16)]),
        out_shape=jax.ShapeDtypeStruct((nb, 16, A_WIDTH), F32),
        compiler_params=_params(1),
        name="dsa_sample_attend",
    )(page_table, q16, keys, thr, *([k_pool] * N_PAGES), k_new, *([v_pool] * N_PAGES), v_new)


def _ret_consts(chunk):
    lg = jnp.log1p(-jnp.exp2(-5.0 - jnp.arange(R_HEADS, dtype=F32)))
    n = jnp.arange(chunk, dtype=F32)
    diff = n[:, None] - n[None, :]
    dmat = jnp.where(diff >= 0, jnp.exp(lg[:, None, None] * jnp.maximum(diff, 0.0)[None]), 0.0)
    cdec = jnp.exp(lg[:, None] * (n[None, :] + 1.0))
    kdec = jnp.exp(lg[:, None] * (chunk - 1.0 - n[None, :]))
    sdec = jnp.exp(lg * chunk)
    return dmat, cdec, kdec, sdec


def _ret_out_store(o, g, o_ref, sl):
    on = o * lax.rsqrt(jnp.mean(o * o, axis=-1, keepdims=True) + RMS_EPS)
    o_ref[:, sl] = (on * (g * jax.nn.sigmoid(g))).astype(o_ref.dtype)


def _ret_prompt_kernel(q_ref, k_ref, v_ref, g_ref, dmat_ref, cdec_ref, kdec_ref, sdec_ref,
                       o_ref, sout_ref, s_ref):
    c = pl.program_id(0)

    @pl.when(c == 0)
    def _():
        s_ref[...] = jnp.zeros_like(s_ref)

    for hh in range(R_HEADS):
        sl = slice(hh * R_DK, (hh + 1) * R_DK)
        k = k_ref[:, sl]
        qb = q_ref[:, sl].astype(BF16)
        vb = v_ref[:, sl].astype(BF16)
        att = _nt_dot(qb, k.astype(BF16)) * dmat_ref[hh]
        intra = _dot(att.astype(BF16), vb)
        s0 = s_ref[hh]
        cross = _dot(qb, s0.astype(BF16)) * cdec_ref[hh]
        kd_t = (k * kdec_ref[hh]).T.astype(BF16)
        s_ref[hh] = sdec_ref[hh] * s0 + _dot(kd_t, vb)
        _ret_out_store(intra + cross, g_ref[:, sl], o_ref, sl)

    @pl.when(c == pl.num_programs(0) - 1)
    def _():
        sout_ref[...] = s_ref[...]


def _retention_prompt(qr, kr, vg):
    t = qr.shape[0]
    c = RET_CHUNK
    dmat, cdec, kdec, sdec = _ret_consts(c)
    rep = lambda a: jnp.broadcast_to(a[:, :, None], (R_HEADS, c, LANES))
    cdec_b, kdec_b = rep(cdec), rep(kdec)
    sdec_b = jnp.broadcast_to(sdec[:, None, None], (R_HEADS, R_DK, LANES))
    blk = lambda col: pl.BlockSpec((c, R_WIDTH), lambda i, col=col: (i, col))
    const = lambda: pl.BlockSpec((R_HEADS, c, LANES), lambda i: (0, 0, 0))
    return pl.pallas_call(
        _ret_prompt_kernel,
        grid=(t // c,),
        in_specs=[blk(0), blk(0), blk(0), blk(1), const(), const(), const(), const()],
        out_specs=[pl.BlockSpec((c, R_WIDTH), lambda i: (i, 0)),
                   pl.BlockSpec((R_HEADS, R_DK, LANES), lambda i: (0, 0, 0))],
        out_shape=[jax.ShapeDtypeStruct((t, R_WIDTH), BF16),
                   jax.ShapeDtypeStruct((R_HEADS, R_DK, 128), F32)],
        scratch_shapes=[pltpu.VMEM((R_HEADS, R_DK, 128), F32)],
        compiler_params=_params(1),
        name="retention_prompt",
    )(qr, kr, vg, vg, dmat, cdec_b, kdec_b, sdec_b)


def _ret_sample_kernel(q_ref, k_ref, v_ref, g_ref, s0_ref, dmat_ref, cdec_ref, kdec_ref, sdec_ref,
                       o_ref, sout_ref, kpad_ref, vpad_ref):
    @pl.when(pl.program_id(0) == 0)
    def _():
        kpad_ref[...] = jnp.zeros_like(kpad_ref)
        vpad_ref[...] = jnp.zeros_like(vpad_ref)

    kpad_ref[0:8, :] = k_ref[...]
    vpad_ref[0:8, :] = v_ref[...]
    for hh in range(R_HEADS):
        sl = slice(hh * R_DK, (hh + 1) * R_DK)
        kp = kpad_ref[:, sl]
        vb = vpad_ref[:, sl].astype(BF16)
        qb = q_ref[:, sl].astype(BF16)
        att = _nt_dot(qb, kp.astype(BF16)) * dmat_ref[hh]
        intra = _dot(att.astype(BF16), vb)
        s0 = s0_ref[hh]
        cross = _dot(qb, s0.astype(BF16)) * cdec_ref[hh]
        kd_t = (kp * kdec_ref[hh]).T.astype(BF16)
        sout_ref[hh] = sdec_ref[hh] * s0 + _dot(kd_t, vb)
        _ret_out_store(intra + cross, g_ref[:, sl], o_ref, sl)


def _retention_sample(qr, kr, vr, gr, state, n_tok):
    nb = qr.shape[0]
    dmat, cdec, kdec, sdec = _ret_consts(n_tok)
    dmat_p = jnp.zeros((R_HEADS, 8, LANES), F32).at[:, :n_tok, :n_tok].set(dmat)
    cdec_p = jnp.zeros((R_HEADS, 8, LANES), F32).at[:, :n_tok, :].set(
        jnp.broadcast_to(cdec[:, :, None], (R_HEADS, n_tok, LANES)))
    kdec_p = jnp.zeros((R_HEADS, LANES, LANES), F32).at[:, :n_tok, :].set(
        jnp.broadcast_to(kdec[:, :, None], (R_HEADS, n_tok, LANES)))
    sdec_b = jnp.broadcast_to(sdec[:, None, None], (R_HEADS, R_DK, LANES))
    tok = lambda: pl.BlockSpec((None, 8, R_WIDTH), lambda b: (b, 0, 0))
    st = lambda: pl.BlockSpec((None, R_HEADS, R_DK, 128), lambda b: (b, 0, 0, 0))
    const = lambda rows: pl.BlockSpec((R_HEADS, rows, LANES), lambda b: (0, 0, 0))
    return pl.pallas_call(
        _ret_sample_kernel,
        grid=(nb,),
        in_specs=[tok(), tok(), tok(), tok(), st(), const(8), const(8), const(LANES), const(R_DK)],
        out_specs=[tok(), st()],
        out_shape=[jax.ShapeDtypeStruct((nb, 8, R_WIDTH), F32),
                   jax.ShapeDtypeStruct(state.shape, F32)],
        scratch_shapes=[pltpu.VMEM((LANES, R_WIDTH), F32), pltpu.VMEM((LANES, R_WIDTH), F32)],
        compiler_params=_params(1),
        name="retention_sample",
    )(qr, kr, vr, gr, state, dmat_p, cdec_p, kdec_p, sdec_b)


def _mem_attn_prompt_kernel(q_ref, k_ref, v_ref, o_ref):
    for hh in range(M_HEADS):
        sl = slice(hh * HEAD_DIM, (hh + 1) * HEAD_DIM)
        s = _nt_dot(q_ref[:, sl], k_ref[:, sl])
        p = jnp.exp2(s - jnp.max(s, axis=1, keepdims=True))
        pn = (p / jnp.sum(p, axis=1, keepdims=True)).astype(BF16)
        o_ref[:, sl] = _dot(pn, v_ref[:, sl]).astype(o_ref.dtype)


def _mem_attn_prompt(qm, mk, mv, tm):
    t = qm.shape[0]
    nm = mk.shape[0]
    return pl.pallas_call(
        _mem_attn_prompt_kernel,
        grid=(t // tm,),
        in_specs=[pl.BlockSpec((tm, M_WIDTH), lambda i: (i, 0)),
                  pl.BlockSpec((nm, M_WIDTH), lambda i: (0, 0)),
                  pl.BlockSpec((nm, M_WIDTH), lambda i: (0, 0))],
        out_specs=pl.BlockSpec((tm, M_WIDTH), lambda i: (i, 0)),
        out_shape=jax.ShapeDtypeStruct((t, M_WIDTH), BF16),
        compiler_params=_params(1),
        name="mem_attn_prompt",
    )(qm, mk, mv)


def _mem_attn_sample_kernel(q_ref, k_ref, v_ref, o_ref):
    nm = k_ref.shape[1] // M_HEADS
    for g in range(q_ref.shape[0]):
        outs = []
        for hh in range(M_HEADS):
            rows = pl.ds(hh, nm, stride=M_HEADS)
            s = _nt_dot(q_ref[g, :, hh * HEAD_DIM:(hh + 1) * HEAD_DIM], k_ref[g, rows, :].astype(BF16))
            p = jnp.exp2(s - jnp.max(s, axis=1, keepdims=True))
            pn = (p / jnp.sum(p, axis=1, keepdims=True)).astype(BF16)
            outs.append(_dot(pn, v_ref[g, rows, :].astype(BF16)))
        o_ref[g] = jnp.concatenate(outs, axis=1)


def _mem_attn_sample(q16, mem_k, mem_v):
    nb, nm = mem_k.shape[0], mem_k.shape[1]
    gb = 4 if nb % 4 == 0 else 1
    return pl.pallas_call(
        _mem_attn_sample_kernel,
        grid=(nb // gb,),
        in_specs=[pl.BlockSpec((gb, 16, M_WIDTH), lambda b: (b, 0, 0)),
                  pl.BlockSpec((gb, nm, HEAD_DIM), lambda b: (b, 0, 0)),
                  pl.BlockSpec((gb, nm, HEAD_DIM), lambda b: (b, 0, 0))],
        out_specs=pl.BlockSpec((gb, 16, M_WIDTH), lambda b: (b, 0, 0)),
        out_shape=jax.ShapeDtypeStruct((nb, 16, M_WIDTH), F32),
        compiler_params=_params(1),
        name="mem_attn_sample",
    )(q16, mem_k, mem_v)


def _merged_kernel(h_ref, oa_ref, ob_ref, om_ref, wg0_ref, wg1_ref, wg2_ref, wa_ref, wb_ref, wm_ref, o_ref):
    h = h_ref[...]
    acc = None
    for wg, o, wbr in ((wg0_ref, oa_ref, wa_ref), (wg1_ref, ob_ref, wb_ref), (wg2_ref, om_ref, wm_ref)):
        term = jax.nn.sigmoid(_dot(h, wg[...])) * _dot(o[...].astype(BF16), wbr[...])
        acc = term if acc is None else acc + term
    o_ref[...] = acc.astype(o_ref.dtype)


def _merged(h, o_a, o_b, o_m, w_gates, w_a, w_b, w_m, tm, tn):
    m = h.shape[0]
    nj = D_MODEL // tn
    row = lambda width: pl.BlockSpec((tm, width), lambda j, i: (i, 0))
    gate = lambda b: pl.BlockSpec((D_MODEL, tn), lambda j, i, b=b: (0, b * nj + j))
    br = lambda width: pl.BlockSpec((width, tn), lambda j, i: (0, j))
    return pl.pallas_call(
        _merged_kernel,
        grid=(nj, m // tm),
        in_specs=[row(D_MODEL), row(A_WIDTH), row(R_WIDTH), row(M_WIDTH), gate(0), gate(1), gate(2),
                  br(A_WIDTH), br(R_WIDTH), br(M_WIDTH)],
        out_specs=pl.BlockSpec((tm, tn), lambda j, i: (i, j)),
        out_shape=jax.ShapeDtypeStruct((m, D_MODEL), BF16),
        compiler_params=_params(2),
        name="merged",
    )(h, o_a, o_b, o_m, w_gates, w_gates, w_gates, w_a, w_b, w_m)


def _outproj_kernel(x_ref, mg_ref, wo_ref, n2_ref, wrt_hi_ref, wrt_lo_ref, x2_ref, h2_ref, rl_ref):
    x2 = x_ref[...] + _dot(mg_ref[...], wo_ref[...])
    x2_ref[...] = x2
    h2 = x2 * lax.rsqrt(jnp.mean(x2 * x2, axis=-1, keepdims=True) + RMS_EPS) * n2_ref[...]
    h2_hi = h2.astype(BF16)
    h2_ref[...] = h2_hi
    h2_lo = (h2 - h2_hi.astype(F32)).astype(BF16)
    rl_ref[...] = (_dot(h2_hi, wrt_hi_ref[...]) + _dot(h2_hi, wrt_lo_ref[...])) + _dot(h2_lo, wrt_hi_ref[...])


def _outproj(x, merged, w_out, norm2, w_rt, tm):
    m = x.shape[0]
    one = pl.Buffered(1)
    row = lambda width: pl.BlockSpec((tm, width), lambda i: (i, 0))
    w_rt_hi = w_rt.astype(BF16)
    w_rt_lo = (w_rt - w_rt_hi.astype(F32)).astype(BF16)
    return pl.pallas_call(
        _outproj_kernel,
        grid=(m // tm,),
        in_specs=[row(D_MODEL), row(D_MODEL),
                  pl.BlockSpec((D_MODEL, D_MODEL), lambda i: (0, 0), pipeline_mode=one),
                  pl.BlockSpec((1, D_MODEL), lambda i: (0, 0)),
                  pl.BlockSpec((D_MODEL, LANES), lambda i: (0, 0), pipeline_mode=one),
                  pl.BlockSpec((D_MODEL, LANES), lambda i: (0, 0), pipeline_mode=one)],
        out_specs=[row(D_MODEL), row(D_MODEL), row(LANES)],
        out_shape=[jax.ShapeDtypeStruct((m, D_MODEL), F32), jax.ShapeDtypeStruct((m, D_MODEL), BF16),
                   jax.ShapeDtypeStruct((m, LANES), F32)],
        compiler_params=_params(1),
        name="outproj",
    )(x, merged, w_out, norm2.reshape(1, D_MODEL), w_rt_hi, w_rt_lo)


def _router_kernel(rl_ref, comb_ref):
    z = rl_ref[...]
    lane = lax.broadcasted_iota(I32, z.shape, 1).astype(F32)
    ninf = -jnp.inf
    big = 1e9
    is_g = lane < N_GROUPS
    gl = jnp.where(is_g, z, ninf)
    gmax = jnp.max(gl, axis=1, keepdims=True)
    gsum = jnp.sum(jnp.where(is_g, jnp.exp(gl - gmax), 0.0), axis=1, keepdims=True)
    p_sel = 1.0 / gsum
    g_sel = jnp.min(jnp.where(gl == gmax, lane, big), axis=1, keepdims=True)
    lo = N_GROUPS + EXP_PER_GROUP * g_sel
    el = jnp.where(lane >= lo, jnp.where(lane < lo + EXP_PER_GROUP, z, ninf), ninf)
    v1 = jnp.max(el, axis=1, keepdims=True)
    i1 = jnp.min(jnp.where(el == v1, lane, big), axis=1, keepdims=True)
    el2 = jnp.where(lane == i1, ninf, el)
    v2 = jnp.max(el2, axis=1, keepdims=True)
    i2 = jnp.min(jnp.where(el2 == v2, lane, big), axis=1, keepdims=True)
    e2 = jnp.exp(v2 - v1)
    w1 = 1.0 / (1.0 + e2)
    w2 = e2 / (1.0 + e2)
    comb_ref[...] = jnp.where(lane == i1, w1 * p_sel, 0.0) + jnp.where(lane == i2, w2 * p_sel, 0.0)


def _router(rl, tm):
    m = rl.shape[0]
    return pl.pallas_call(
        _router_kernel,
        grid=(m // tm,),
        in_specs=[pl.BlockSpec((tm, LANES), lambda i: (i, 0))],
        out_specs=pl.BlockSpec((tm, LANES), lambda i: (i, 0)),
        out_shape=jax.ShapeDtypeStruct((m, LANES), F32),
        compiler_params=_params(1),
        name="router",
    )(rl)


def _moe_kernel(x2_ref, h2_ref, comb_ref, wg_ref, wu_ref, wd_ref, y_ref, acc_ref, *, ne):
    k = pl.program_id(1)

    @pl.when(k == 0)
    def _():
        acc_ref[...] = x2_ref[...]

    h2 = h2_ref[...]
    comb = comb_ref[...]
    lane = lax.broadcasted_iota(I32, comb.shape, 1)
    for e in range(ne):
        ce = jnp.sum(jnp.where(lane == N_GROUPS + k * ne + e, comb, 0.0), axis=1, keepdims=True)
        a = _dot(h2, wg_ref[e])
        u = _dot(h2, wu_ref[e])
        act = (a * jax.nn.sigmoid(a)) * u * ce
        acc_ref[...] += _dot(act.astype(BF16), wd_ref[e])

    @pl.when(k == pl.num_programs(1) - 1)
    def _():
        y_ref[...] = acc_ref[...]


def _moe(x2, h2, comb, wg, wu, wd, tm, ne):
    m = x2.shape[0]
    row = lambda width: pl.BlockSpec((tm, width), lambda i, k: (i, 0))
    return pl.pallas_call(
        functools.partial(_moe_kernel, ne=ne),
        grid=(m // tm, N_EXPERTS // ne),
        in_specs=[row(D_MODEL), row(D_MODEL), row(LANES),
                  pl.BlockSpec((ne, D_MODEL, EXP_FF), lambda i, k: (k, 0, 0)),
                  pl.BlockSpec((ne, D_MODEL, EXP_FF), lambda i, k: (k, 0, 0)),
                  pl.BlockSpec((ne, EXP_FF, D_MODEL), lambda i, k: (k, 0, 0))],
        out_specs=row(D_MODEL),
        out_shape=jax.ShapeDtypeStruct((m, D_MODEL), F32),
        scratch_shapes=[pltpu.VMEM((tm, D_MODEL), F32)],
        compiler_params=_params(2),
        name="moe",
    )(x2, h2, comb, wg, wu, wd)


def _rope_tables(pos):
    half = R_DK // 2
    freqs = ROPE_BASE ** (-jnp.arange(half, dtype=F32) / half)
    ang = pos.astype(F32)[:, None] * freqs[None, :]
    cos, sin = jnp.cos(ang), jnp.sin(ang)
    return jnp.concatenate([cos, cos], axis=1), jnp.concatenate([-sin, sin], axis=1)


def _layer_weights(l, w_in, w_mem_kv, w_br_a, w_br_b, w_br_m, w_out, w_group, w_router,
                   w_gate_e, w_up_e, w_down_e):
    wi = w_in[l]
    col = lambda a, b: wi[:, a:b].astype(BF16)
    w_kw = jnp.pad(wi[:, O_KI:O_QR], ((0, 0), (0, LANES - (O_QR - O_KI)))).astype(BF16)
    w_rt = jnp.pad(jnp.concatenate([w_group[l], w_router[l]], axis=1),
                   ((0, 0), (0, LANES - N_GROUPS - N_EXPERTS)))
    return dict(
        qa=col(O_QA, O_KA), ka=col(O_KA, O_VA), va=col(O_VA, O_QI), qi=col(O_QI, O_KI), kw=w_kw,
        qr=col(O_QR, O_KR), kr=col(O_KR, O_VR), vg=col(O_VR, O_QM), qm=col(O_QM, O_GATES),
        gates=col(O_GATES, O_END),
        mk=w_mem_kv[l][:, :M_WIDTH].astype(BF16), mv=w_mem_kv[l][:, M_WIDTH:].astype(BF16),
        br_a=w_br_a[l].astype(BF16), br_b=w_br_b[l].astype(BF16), br_m=w_br_m[l].astype(BF16),
        out=w_out[l].astype(BF16), rt=w_rt,
        ge=w_gate_e[l].astype(BF16), ue=w_up_e[l].astype(BF16), de=w_down_e[l].astype(BF16))


def _project(x2d, pos, w, norms, tm):
    t = x2d.shape[0]
    norm1, q_norm_a, k_norm_a, idx_k_norm, q_norm_m = norms
    h = _rmsnorm_cast(x2d, norm1, tm)
    hd = lambda g: g.reshape(1, HEAD_DIM)
    gspec = _const_spec((1, HEAD_DIM))
    scale = HEAD_DIM ** -0.5 * LOG2_E
    (qa,) = _proj(functools.partial(_proj_headnorm_kernel, scale=scale), h, w["qa"], [hd(q_norm_a)], [gspec],
                  [(BF16, ROW)], tm, A_WIDTH, "proj_qa")
    ka, kab = _proj(functools.partial(_proj_headnorm_kernel, scale=1.0), h, w["ka"], [hd(k_norm_a)], [gspec],
                    [(F32, HEAD_MAJOR), (BF16, HEAD_MAJOR)], tm, A_WIDTH, "proj_ka")
    va, vab = _proj(_proj_plain_kernel, h, w["va"], [], [], [(F32, HEAD_MAJOR), (BF16, HEAD_MAJOR)], tm, A_WIDTH,
                    "proj_va")
    (qi,) = _proj(_proj_plain_kernel, h, w["qi"], [], [], [(BF16, ROW)], tm, 512, "proj_qi")
    g_idx = jnp.pad(idx_k_norm, (0, LANES - IDX_DIM)).reshape(1, LANES)
    kw, kt, kte, kto = _proj(_proj_kw_kernel, h, w["kw"], [g_idx], [_const_spec((1, LANES))],
                             [(F32, ROW), (F32, TRANSPOSED), (BF16, TRANSPOSED), (BF16, TRANSPOSED)], tm, LANES,
                             "proj_kw")
    cos2, sin2 = _rope_tables(pos)
    tabs, tab_specs = [cos2, sin2], [_row_spec(tm, R_DK), _row_spec(tm, R_DK)]
    (qr,) = _proj(functools.partial(_proj_rope_kernel, scale=1.0), h, w["qr"], tabs, tab_specs,
                  [(F32, ROW)], tm, R_WIDTH, "proj_qr")
    (kr,) = _proj(functools.partial(_proj_rope_kernel, scale=R_DK ** -0.5), h, w["kr"], tabs, tab_specs,
                  [(F32, ROW)], tm, R_WIDTH, "proj_kr")
    (vg,) = _proj(_proj_plain_kernel, h, w["vg"], [], [], [(F32, ROW)], tm, R_WIDTH, "proj_vg")
    (qm,) = _proj(functools.partial(_proj_headnorm_kernel, scale=scale), h, w["qm"], [hd(q_norm_m)], [gspec],
                  [(BF16, ROW)], tm, M_WIDTH, "proj_qm")
    return dict(h=h, qa=qa, ka=ka, kab=kab, va=va, vab=vab, qi=qi, kw=kw, kt=kt, kte=kte, kto=kto,
                qr=qr, kr=kr, vg=vg, qm=qm)


def _finish(x2d, h, o_a, o_b, o_m, w, norm2, tm):
    merged = _merged(h, o_a, o_b, o_m, w["gates"], w["br_a"], w["br_b"], w["br_m"], tm, 512)
    x2, h2, rl = _outproj(x2d, merged, w["out"], norm2, w["rt"], min(tm, 256))
    comb = _router(rl, tm)
    return _moe(x2, h2, comb, w["ge"], w["ue"], w["de"], tm, 2)


def kernel(x_prompt, x_sample, mem_prompt, cache_k, cache_v, cache_idx_k, state_ret, cache_mem_k, cache_mem_v,
           page_table, norm1, w_in, q_norm_a, k_norm_a, idx_k_norm, q_norm_m, k_norm_m, mem_norm, w_mem_kv,
           w_br_a, w_br_b, w_br_m, w_out, norm2, w_group, w_router, w_gate_e, w_up_e, w_down_e):
    depth = w_in.shape[0]
    bp, seq, _ = x_prompt.shape
    nb, n_tok, _ = x_sample.shape
    assert bp == 1 and n_tok <= 8
    y_p = x_prompt.reshape(seq, D_MODEL)
    y_s = x_sample.reshape(nb * n_tok, D_MODEL)
    pos_p = jnp.arange(seq)
    pos_s = jnp.tile(PAST_LEN + jnp.arange(n_tok), nb)
    outs = [[] for _ in range(10)]
    for l in range(depth):
        w = _layer_weights(l, w_in, w_mem_kv, w_br_a, w_br_b, w_br_m, w_out, w_group, w_router,
                           w_gate_e, w_up_e, w_down_e)
        norms = (norm1[l], q_norm_a[l], k_norm_a[l], idx_k_norm[l], q_norm_m[l])

        tm = 512
        p = _project(y_p, pos_p, w, norms, 2 * tm)
        o_a = _dsa_prompt(p["qa"], p["qi"], p["kw"], p["kte"], p["kto"], p["kab"], p["vab"])
        o_r, s_fin = _retention_prompt(p["qr"], p["kr"], p["vg"])
        n_mem = mem_prompt.shape[1]
        hm = _rmsnorm_cast(mem_prompt.reshape(n_mem, D_MODEL), mem_norm[l], n_mem)
        mk, mkb = _proj(functools.partial(_proj_headnorm_kernel, scale=1.0), hm, w["mk"],
                        [k_norm_m[l].reshape(1, HEAD_DIM)], [_const_spec((1, HEAD_DIM))],
                        [(F32, ROW), (BF16, ROW)], n_mem, M_WIDTH, "proj_mk")
        mv, mvb = _proj(_proj_plain_kernel, hm, w["mv"], [], [], [(F32, ROW), (BF16, ROW)],
                        n_mem, M_WIDTH, "proj_mv")
        o_m = _mem_attn_prompt(p["qm"], mkb, mvb, tm)
        y_p = _finish(y_p, p["h"], o_a, o_r, o_m, w, norm2[l], tm)
        outs[0].append(jnp.transpose(p["ka"], (1, 0, 2)).reshape(bp, seq, A_HEADS, HEAD_DIM))
        outs[1].append(jnp.transpose(p["va"], (1, 0, 2)).reshape(bp, seq, A_HEADS, HEAD_DIM))
        outs[2].append(p["kt"][:IDX_DIM].T.reshape(bp, seq, IDX_DIM))
        outs[3].append(s_fin.reshape(bp, R_HEADS, R_DK, 128))
        outs[4].append(mk.reshape(bp, n_mem, M_HEADS, HEAD_DIM))
        outs[5].append(mv.reshape(bp, n_mem, M_HEADS, HEAD_DIM))

        ts = nb * n_tok
        s = _project(y_s, pos_s, w, norms, ts)
        pad_tok = lambda a: jnp.pad(a.reshape(nb, n_tok, -1), ((0, 0), (0, 8 - n_tok), (0, 0)))
        a_idx = s["qi"].reshape(nb, n_tok * IDX_HEADS, IDX_DIM)
        w_idx = s["kw"][:, IDX_DIM:IDX_DIM + IDX_HEADS].reshape(nb, n_tok * IDX_HEADS, 1)
        pad16 = lambda a: jnp.pad(a.reshape(nb, n_tok, -1), ((0, 0), (0, 16 - n_tok), (0, 0)))
        ki_new = jnp.pad(jnp.transpose(s["kte"][:IDX_DIM].reshape(IDX_DIM, nb, n_tok), (1, 0, 2)),
                         ((0, 0), (0, 0), (0, PAGE_SIZE - n_tok)))
        new_page = lambda a: jnp.pad(jnp.transpose(a.reshape(A_HEADS, nb, n_tok, HEAD_DIM), (1, 0, 2, 3)),
                                     ((0, 0), (0, 0), (0, PAGE_SIZE - n_tok), (0, 0)))
        head_major_pool = lambda c: jnp.transpose(c, (0, 2, 1, 3))
        o_a_s = _dsa_sample(page_table, a_idx, w_idx, pad16(s["qa"]),
                            jnp.swapaxes(cache_idx_k[l], 1, 2), ki_new,
                            head_major_pool(cache_k[l]), new_page(s["kab"]),
                            head_major_pool(cache_v[l]), new_page(s["vab"]), n_tok)
        o_r_s, s_new = _retention_sample(pad_tok(s["qr"]), pad_tok(s["kr"]), pad_tok(s["vg"][:, :R_WIDTH]),
                                         pad_tok(s["vg"][:, R_WIDTH:]), state_ret[l], n_tok)
        mem_rows = lambda c: c.reshape(nb, -1, HEAD_DIM)
        o_m_s = _mem_attn_sample(pad16(s["qm"]), mem_rows(cache_mem_k[l]), mem_rows(cache_mem_v[l]))
        unpad = lambda a: a[:, :n_tok, :].reshape(ts, -1)
        y_s = _finish(y_s, s["h"], unpad(o_a_s), unpad(o_r_s), unpad(o_m_s), w, norm2[l], ts)
        outs[6].append(jnp.transpose(s["ka"], (1, 0, 2)).reshape(nb, n_tok, A_HEADS, HEAD_DIM))
        outs[7].append(jnp.transpose(s["va"], (1, 0, 2)).reshape(nb, n_tok, A_HEADS, HEAD_DIM))
        outs[8].append(s["kt"][:IDX_DIM].T.reshape(nb, n_tok, IDX_DIM))
        outs[9].append(s_new)
    stk = [jnp.stack(o) for o in outs]
    return (y_p.reshape(bp, seq, D_MODEL), y_s.reshape(nb, n_tok, D_MODEL), *stk)
```
